```python
import jax, jax.numpy as jnp
from jax import lax
import numpy as np

D_MODEL = 2048
BATCH = 2
SEQ = 16384
DEPTH = 2

GRID_W = 64
CTX_LEN = 256
HEAD_DIM = 128
N_Q_HEADS = 12
N_KV_HEADS = 4
GQA_GROUP = N_Q_HEADS // N_KV_HEADS
WINDOW = 128
BLOCK = WINDOW
ATTN_WIDTH = N_Q_HEADS * HEAD_DIM
KV_WIDTH = N_KV_HEADS * HEAD_DIM
FOURIER_WIDTH = D_MODEL - ATTN_WIDTH
FOURIER_GROUPS = 4
FOURIER_GROUP_DIM = FOURIER_WIDTH // FOURIER_GROUPS
MIX0_WIDTH = ATTN_WIDTH + FOURIER_WIDTH
SPLIT0 = (ATTN_WIDTH, ATTN_WIDTH + KV_WIDTH, ATTN_WIDTH + 2 * KV_WIDTH,
          ATTN_WIDTH + 2 * KV_WIDTH + FOURIER_WIDTH)
IN0_WIDTH = ATTN_WIDTH + 2 * KV_WIDTH + FOURIER_WIDTH + MIX0_WIDTH
POOL_WINDOWS = (2, 4, 8, 16)
POOL_GROUPS = len(POOL_WINDOWS)
POOL_WIDTH = D_MODEL
POOL_GROUP_DIM = POOL_WIDTH // POOL_GROUPS
IN1_WIDTH = 2 * POOL_WIDTH
ROPE_BASE = 10000.0
ROPE_PAIRS_PER_AXIS = HEAD_DIM // 4
RMS_EPS = 1e-6

kernel_name = "hybrid_swa_fourier_pool_prefix_dit"


def rmsnorm(x, g):
    xf = x.astype(jnp.float32)
    y = xf * lax.rsqrt(jnp.mean(xf * xf, axis=-1, keepdims=True) + RMS_EPS)
    return (y * g.astype(jnp.float32)).astype(x.dtype)


def adaln(cond, w_mod, b_mod):
    m = jax.nn.silu(cond) @ w_mod + b_mod
    return jnp.split(m, 3, axis=-1)


def axial_rope(rows):
    row = jnp.repeat(jnp.arange(rows, dtype=jnp.float32), GRID_W)
    col = jnp.tile(jnp.arange(GRID_W, dtype=jnp.float32), rows)
    inv = 1.0 / (ROPE_BASE ** (jnp.arange(ROPE_PAIRS_PER_AXIS, dtype=jnp.float32) / ROPE_PAIRS_PER_AXIS))
    ang = jnp.stack([row[:, None] * inv, col[:, None] * inv], axis=1)
    return jnp.cos(ang), jnp.sin(ang)


def apply_rope(x, cos, sin):
    B, S, H, _ = x.shape
    xr = x.reshape(B, S, H, 2, 2, ROPE_PAIRS_PER_AXIS).astype(jnp.float32)
    x1, x2 = xr[..., 0, :], xr[..., 1, :]
    cs, sn = cos[None, :, None], sin[None, :, None]
    out = jnp.stack([x1 * cs - x2 * sn, x2 * cs + x1 * sn], axis=-2)
    return out.reshape(B, S, H, HEAD_DIM).astype(x.dtype)


def sink_logits(sink, lead_shape):
    s = sink.astype(jnp.float32).reshape(1, N_KV_HEADS, GQA_GROUP, 1, 1)
    return jnp.broadcast_to(s, lead_shape + (1,))


def windowed_attention(q, k, v, k_ctx, v_ctx, sink):
    B, S = q.shape[:2]
    C = k_ctx.shape[1]
    nblk = S // BLOCK
    scale = HEAD_DIM ** -0.5
    pad = ((0, 0), (WINDOW, WINDOW), (0, 0), (0, 0))
    kp = jnp.pad(k, pad)
    vp = jnp.pad(v, pad)
    qb = q.reshape(B, nblk, BLOCK, N_KV_HEADS, GQA_GROUP, HEAD_DIM).transpose(1, 0, 2, 3, 4, 5)

    def attend_block(args):
        n, qn = args
        kw = lax.dynamic_slice_in_dim(kp, n * BLOCK, BLOCK + 2 * WINDOW, axis=1)
        vw = lax.dynamic_slice_in_dim(vp, n * BLOCK, BLOCK + 2 * WINDOW, axis=1)
        s_win = jnp.einsum('bqkgd,bjkd->bkgqj', qn, kw).astype(jnp.float32) * scale
        s_ctx = jnp.einsum('bqkgd,bjkd->bkgqj', qn, k_ctx).astype(jnp.float32) * scale
        kpos = n * BLOCK - WINDOW + jnp.arange(BLOCK + 2 * WINDOW)
        qpos = n * BLOCK + jnp.arange(BLOCK)
        valid = (jnp.abs(kpos[None, :] - qpos[:, None]) <= WINDOW) & (kpos >= 0)[None, :] & (kpos < S)[None, :]
        s_win = jnp.where(valid, s_win, -jnp.inf)
        logits = jnp.concatenate([sink_logits(sink, s_win.shape[:-1]), s_ctx, s_win], axis=-1)
        p = jax.nn.softmax(logits, axis=-1)
        p_ctx = p[..., 1:1 + C].astype(v.dtype)
        p_win = p[..., 1 + C:].astype(v.dtype)
        return (jnp.einsum('bkgqj,bjkd->bqkgd', p_ctx, v_ctx)
                + jnp.einsum('bkgqj,bjkd->bqkgd', p_win, vw))

    out = lax.map(attend_block, (jnp.arange(nblk), qb))
    return out.transpose(1, 0, 2, 3, 4, 5).reshape(B, S, ATTN_WIDTH)


def context_attention(q, k, v, sink):
    B, C = q.shape[:2]
    qg = q.reshape(B, C, N_KV_HEADS, GQA_GROUP, HEAD_DIM)
    s = jnp.einsum('bqkgd,bjkd->bkgqj', qg, k).astype(jnp.float32) * (HEAD_DIM ** -0.5)
    logits = jnp.concatenate([sink_logits(sink, s.shape[:-1]), s], axis=-1)
    p = jax.nn.softmax(logits, axis=-1)[..., 1:].astype(v.dtype)
    return jnp.einsum('bkgqj,bjkd->bqkgd', p, v).reshape(B, C, ATTN_WIDTH)


def fourier_mix(f, w_f):
    B, N, _ = f.shape
    fg = f.reshape(B, N, FOURIER_GROUPS, FOURIER_GROUP_DIM).astype(jnp.float32)
    z = jnp.fft.fft2(fg, axes=(1, 3), norm="ortho").real.astype(f.dtype)
    return jnp.einsum('bngc,gcd->bngd', z, w_f).reshape(B, N, FOURIER_WIDTH)


def multiscale_pool(u, w_grp, layer_scale):
    B, N, _ = u.shape
    ug = u.reshape(B, N, POOL_GROUPS, POOL_GROUP_DIM)
    cs = jnp.concatenate([jnp.zeros((B, 1, POOL_GROUPS, POOL_GROUP_DIM), jnp.float32),
                          jnp.cumsum(ug.astype(jnp.float32), axis=1)], axis=1)
    t = jnp.arange(N)[:, None]
    w = jnp.array(POOL_WINDOWS, dtype=jnp.int32)[None, :]
    lo = jnp.clip(t - w // 2, 0, N)
    hi = jnp.clip(t - w // 2 + w, 0, N)
    gidx = jnp.arange(POOL_GROUPS)[None, :]
    win_sum = cs[:, hi, gidx, :] - cs[:, lo, gidx, :]
    mean = win_sum / (hi - lo).astype(jnp.float32)[None, :, :, None]
    pooled = (mean - ug.astype(jnp.float32)).astype(u.dtype)
    y = jnp.einsum('bngc,gcd->bngd', pooled, w_grp).reshape(B, N, POOL_WIDTH)
    return y * layer_scale


def attn_fourier_layer(x, ctx, c, c_ctx, cos, sin, norm_g, w_mod, b_mod, w_in, sink, w_f, w_out, need_ctx_out):
    B, S, _ = x.shape
    C = ctx.shape[1]
    shift, scale, gate = adaln(c, w_mod, b_mod)
    shift_c, scale_c, gate_c = adaln(c_ctx, w_mod, b_mod)
    h = rmsnorm(x, norm_g) * (1.0 + scale[:, None]) + shift[:, None]
    hc = rmsnorm(ctx, norm_g) * (1.0 + scale_c) + shift_c

    q, k, v, f, z = jnp.split(h @ w_in, SPLIT0, axis=-1)
    q = apply_rope(q.reshape(B, S, N_Q_HEADS, HEAD_DIM), cos, sin)
    k = apply_rope(k.reshape(B, S, N_KV_HEADS, HEAD_DIM), cos, sin)
    v = v.reshape(B, S, N_KV_HEADS, HEAD_DIM)

    if need_ctx_out:
        qc, kc, vc, fc, zc = jnp.split(hc @ w_in, SPLIT0, axis=-1)
    else:
        kc, vc = jnp.split(hc @ w_in[:, ATTN_WIDTH:ATTN_WIDTH + 2 * KV_WIDTH], 2, axis=-1)
    kc = kc.reshape(B, C, N_KV_HEADS, HEAD_DIM)
    vc = vc.reshape(B, C, N_KV_HEADS, HEAD_DIM)

    y = jnp.concatenate([windowed_attention(q, k, v, kc, vc, sink), fourier_mix(f, w_f)], axis=-1)
    x = x + gate[:, None] * ((y * jax.nn.silu(z)) @ w_out)

    if need_ctx_out:
        qc = qc.reshape(B, C, N_Q_HEADS, HEAD_DIM)
        yc = jnp.concatenate([context_attention(qc, kc, vc, sink), fourier_mix(fc, w_f)], axis=-1)
        ctx = ctx + gate_c * ((yc * jax.nn.silu(zc)) @ w_out)
    return x, ctx


def pool_branch(h, w_in, w_grp, layer_scale, w_out):
    u, z = jnp.split(h @ w_in, 2, axis=-1)
    return (multiscale_pool(u, w_grp, layer_scale) * jax.nn.silu(z)) @ w_out


def pool_layer(x, ctx, c, c_ctx, norm_g, w_mod, b_mod, w_in, w_grp, layer_scale, w_out, need_ctx_out):
    shift, scale, gate = adaln(c, w_mod, b_mod)
    h = rmsnorm(x, norm_g) * (1.0 + scale[:, None]) + shift[:, None]
    x = x + gate[:, None] * pool_branch(h, w_in, w_grp, layer_scale, w_out)
    if need_ctx_out:
        shift_c, scale_c, gate_c = adaln(c_ctx, w_mod, b_mod)
        hc = rmsnorm(ctx, norm_g) * (1.0 + scale_c) + shift_c
        ctx = ctx + gate_c * pool_branch(hc, w_in, w_grp, layer_scale, w_out)
    return x, ctx


def setup_inputs(seed: int = 0) -> dict:
    key = jax.random.key(seed)
    ks = jax.random.split(key, 20)
    D = D_MODEL
    nrm = jax.random.normal
    f32 = jnp.float32
    return {
        "x": nrm(ks[0], (BATCH, SEQ, D), f32),
        "c": nrm(ks[1], (BATCH, D), f32),
        "ctx": nrm(ks[2], (BATCH, CTX_LEN, D), f32),
        "c_ctx": nrm(ks[3], (D,), f32),
        "l0_norm": 1.0 + 0.1 * nrm(ks[4], (D,), f32),
        "l0_w_mod": nrm(ks[5], (D, 3 * D), f32) * (0.5 * D ** -0.5),
        "l0_b_mod": 0.02 * nrm(ks[6], (3 * D,), f32),
        "l0_w_in": nrm(ks[7], (D, IN0_WIDTH), f32) * D ** -0.5,
        "l0_sink": nrm(ks[8], (N_Q_HEADS,), f32),
        "l0_w_f": nrm(ks[9], (FOURIER_GROUPS, FOURIER_GROUP_DIM, FOURIER_GROUP_DIM), f32) * FOURIER_GROUP_DIM ** -0.5,
        "l0_w_out": nrm(ks[10], (MIX0_WIDTH, D), f32) * MIX0_WIDTH ** -0.5,
        "l1_norm": 1.0 + 0.1 * nrm(ks[11], (D,), f32),
        "l1_w_mod": nrm(ks[12], (D, 3 * D), f32) * (0.5 * D ** -0.5),
        "l1_b_mod": 0.02 * nrm(ks[13], (3 * D,), f32),
        "l1_w_in": nrm(ks[14], (D, IN1_WIDTH), f32) * D ** -0.5,
        "l1_w_grp": nrm(ks[15], (POOL_GROUPS, POOL_GROUP_DIM, POOL_GROUP_DIM), f32) * POOL_GROUP_DIM ** -0.5,
        "l1_scale": 1.0 + 0.1 * nrm(ks[16], (POOL_WIDTH,), f32),
        "l1_w_out": nrm(ks[17], (POOL_WIDTH, D), f32) * POOL_WIDTH ** -0.5,
        "final_norm": 1.0 + 0.1 * nrm(ks[18], (D,), f32),
    }


def reference(x, c, ctx, c_ctx,
              l0_norm, l0_w_mod, l0_b_mod, l0_w_in, l0_sink, l0_w_f, l0_w_out,
              l1_norm, l1_w_mod, l1_b_mod, l1_w_in, l1_w_grp, l1_scale, l1_w_out,
              final_norm):
    ROWS = x.shape[1] // GRID_W
    cos, sin = axial_rope(ROWS)
    layer_params = [
        (l0_norm, l0_w_mod, l0_b_mod, l0_w_in, l0_sink, l0_w_f, l0_w_out),
        (l1_norm, l1_w_mod, l1_b_mod, l1_w_in, l1_w_grp, l1_scale, l1_w_out),
    ]
    for i in range(DEPTH):
        need_ctx_out = any(j % 2 == 0 for j in range(i + 1, DEPTH))
        if i % 2 == 0:
            x, ctx = attn_fourier_layer(x, ctx, c, c_ctx, cos, sin, *layer_params[i], need_ctx_out)
        else:
            x, ctx = pool_layer(x, ctx, c, c_ctx, *layer_params[i], need_ctx_out)
    return rmsnorm(x, final_norm)
```

```python
import functools
import math

import numpy as np
import jax
import jax.numpy as jnp
from jax import lax
from jax.experimental import pallas as pl
from jax.experimental.pallas import tpu as pltpu

F32 = jnp.float32
BF16 = jnp.bfloat16

D_MODEL = 2048
GRID_W = 64
HEAD_DIM = 128
N_Q_HEADS = 12
N_KV_HEADS = 4
GQA_GROUP = N_Q_HEADS // N_KV_HEADS
WINDOW = 128
ATTN_WIDTH = N_Q_HEADS * HEAD_DIM
KV_WIDTH = N_KV_HEADS * HEAD_DIM
FOURIER_WIDTH = D_MODEL - ATTN_WIDTH
FOURIER_GROUPS = 4
FOURIER_GROUP_DIM = FOURIER_WIDTH // FOURIER_GROUPS
IN0_WIDTH = ATTN_WIDTH + 2 * KV_WIDTH + FOURIER_WIDTH + D_MODEL
POOL_WINDOWS = (2, 4, 8, 16)
POOL_GROUPS = len(POOL_WINDOWS)
POOL_GROUP_DIM = D_MODEL // POOL_GROUPS
ROPE_BASE = 10000.0
ROPE_PAIRS = HEAD_DIM // 4
RMS_EPS = 1e-6

LANES = 128
SUBLANES = 8
BF16_SUBLANES = 16
VMEM_LIMIT_BYTES = 56 * 1024 * 1024

FFT_RADIX = 128
FFT_PITCH = FFT_RADIX + SUBLANES

COND_ROWS = 8


def _silu(x):
    return x * jax.nn.sigmoid(x)


def _compiler_params(semantics):
    return pltpu.CompilerParams(dimension_semantics=semantics, vmem_limit_bytes=VMEM_LIMIT_BYTES)


def _resident(block_shape, index_map):
    return pl.BlockSpec(block_shape, index_map, pipeline_mode=pl.Buffered(1))


@functools.lru_cache(maxsize=None)
def _rope_tables(seq):
    pos = np.arange(seq)
    row = (pos // GRID_W).astype(np.float64)
    col = (pos % GRID_W).astype(np.float64)
    inv = 1.0 / (ROPE_BASE ** (np.arange(ROPE_PAIRS, dtype=np.float64) / ROPE_PAIRS))
    ar, ac = row[:, None] * inv, col[:, None] * inv
    cos = np.concatenate([np.cos(ar), np.cos(ar), np.cos(ac), np.cos(ac)], axis=1)
    sin = np.concatenate([-np.sin(ar), np.sin(ar), -np.sin(ac), np.sin(ac)], axis=1)
    return cos.astype(np.float32), sin.astype(np.float32)


@functools.lru_cache(maxsize=None)
def _dft_tables():
    r = FFT_RADIX
    idx = np.arange(r)
    ang = 2.0 * np.pi * np.outer(idx, idx) / r
    c, s = np.cos(ang) / math.sqrt(r), np.sin(ang) / math.sqrt(r)
    stage1 = np.concatenate([c, -s], axis=0)
    chan = np.concatenate([c, -s], axis=0)
    v = idx[:, None, None]
    u = idx[None, :, None]
    n1 = idx[None, None, :]
    k = (n1 * (r * u + v)) % (r * r)
    ang2 = 2.0 * np.pi * k / (r * r)
    stage2 = np.concatenate([np.cos(ang2), np.sin(ang2)], axis=2) / math.sqrt(r)
    return stage1.astype(np.float32), chan.astype(np.float32), stage2.astype(np.float32)


@functools.lru_cache(maxsize=None)
def _pool_edge_tables(seq):
    first = np.zeros((SUBLANES, D_MODEL), np.float64)
    last = np.zeros((SUBLANES, D_MODEL), np.float64)
    for g, w in enumerate(POOL_WINDOWS):
        for r in range(SUBLANES):
            for tab, t in ((first, r), (last, seq - SUBLANES + r)):
                lo = min(max(t - w // 2, 0), seq)
                hi = min(max(t - w // 2 + w, 0), seq)
                tab[r, g * POOL_GROUP_DIM:(g + 1) * POOL_GROUP_DIM] = 1.0 / (hi - lo)
    interior = np.repeat(1.0 / np.array(POOL_WINDOWS, np.float64), POOL_GROUP_DIM)[None, :]
    return first.astype(np.float32), last.astype(np.float32), interior.astype(np.float32)


def _mod_kernel(cond_ref, w_ref, b_ref, o_ref):
    a = _silu(cond_ref[...]).astype(BF16)
    o_ref[...] = jnp.dot(a, w_ref[...].astype(BF16), preferred_element_type=F32) + b_ref[...]


def _modulation(cond, w_mod, b_mod):
    d, n = w_mod.shape
    tn = 1536
    return pl.pallas_call(
        _mod_kernel,
        grid=(n // tn,),
        in_specs=[
            pl.BlockSpec((COND_ROWS, d), lambda j: (0, 0)),
            pl.BlockSpec((d, tn), lambda j: (0, j)),
            pl.BlockSpec((1, tn), lambda j: (0, j)),
        ],
        out_specs=pl.BlockSpec((COND_ROWS, tn), lambda j: (0, j)),
        out_shape=jax.ShapeDtypeStruct((COND_ROWS, n), F32),
        compiler_params=_compiler_params(("arbitrary",)),
        name="modulation",
    )(cond, w_mod, b_mod.reshape(1, n))


def _norm_mod(x, g, scale, shift):
    ms = jnp.mean(x * x, axis=-1, keepdims=True)
    return x * lax.rsqrt(ms + RMS_EPS) * g * (1.0 + scale) + shift


IN0_CHUNK = 512


def _rope(x, cos, sin, partner_is_right):
    rot = jnp.where(partner_is_right, pltpu.roll(x, HEAD_DIM - ROPE_PAIRS, 1), pltpu.roll(x, ROPE_PAIRS, 1))
    return x * cos + rot * sin


def _inproj0_kernel(x_ref, g_ref, scale_ref, shift_ref, w_ref, cos_ref, sin_ref,
                    q_ref, k_ref, v_ref, f_ref, sz_ref, h_scr):
    h_scr[...] = _norm_mod(x_ref[0], g_ref[...], scale_ref[0], shift_ref[0]).astype(BF16)
    cos = cos_ref[...]
    sin = sin_ref[...]
    q_scale = HEAD_DIM ** -0.5
    cos_q, sin_q = cos * q_scale, sin * q_scale
    lane = lax.broadcasted_iota(jnp.int32, cos.shape, 1)
    partner_is_right = (lane % (2 * ROPE_PAIRS)) < ROPE_PAIRS
    heads_per_chunk = IN0_CHUNK // HEAD_DIM

    for c0 in range(0, IN0_WIDTH, IN0_CHUNK):
        acc = jnp.dot(h_scr[...], w_ref[:, c0:c0 + IN0_CHUNK], preferred_element_type=F32)
        if c0 < ATTN_WIDTH + KV_WIDTH:
            is_q = c0 < ATTN_WIDTH
            dst, off = (q_ref, c0) if is_q else (k_ref, c0 - ATTN_WIDTH)
            cs, sn = (cos_q, sin_q) if is_q else (cos, sin)
            for j in range(heads_per_chunk):
                xh = acc[:, j * HEAD_DIM:(j + 1) * HEAD_DIM]
                lo = off + j * HEAD_DIM
                dst[0, :, lo:lo + HEAD_DIM] = _rope(xh, cs, sn, partner_is_right).astype(dst.dtype)
        elif c0 < ATTN_WIDTH + 2 * KV_WIDTH:
            v_ref[0] = acc.astype(v_ref.dtype)
        elif c0 < ATTN_WIDTH + 2 * KV_WIDTH + FOURIER_WIDTH:
            f_ref[0] = acc
        else:
            off = c0 - (IN0_WIDTH - D_MODEL)
            sz_ref[0, :, off:off + IN0_CHUNK] = _silu(acc).astype(sz_ref.dtype)


def _inproj0(x, norm_g, scale, shift, w_in, cos, sin, tm=512):
    b, s, d = x.shape
    vec = pl.BlockSpec((1, 1, d), lambda bi, i: (bi, 0, 0))
    tok = lambda width: pl.BlockSpec((1, tm, width), lambda bi, i: (bi, i, 0))
    tab = pl.BlockSpec((tm, HEAD_DIM), lambda bi, i: (i, 0))
    return pl.pallas_call(
        _inproj0_kernel,
        grid=(b, s // tm),
        in_specs=[tok(d), _resident((1, d), lambda bi, i: (0, 0)), vec, vec,
                  _resident((d, IN0_WIDTH), lambda bi, i: (0, 0)), tab, tab],
        out_specs=[tok(ATTN_WIDTH), tok(KV_WIDTH), tok(KV_WIDTH), tok(FOURIER_WIDTH), tok(D_MODEL)],
        out_shape=[
            jax.ShapeDtypeStruct((b, s, ATTN_WIDTH), BF16),
            jax.ShapeDtypeStruct((b, s, KV_WIDTH), BF16),
            jax.ShapeDtypeStruct((b, s, KV_WIDTH), BF16),
            jax.ShapeDtypeStruct((b, s, FOURIER_WIDTH), F32),
            jax.ShapeDtypeStruct((b, s, D_MODEL), BF16),
        ],
        scratch_shapes=[pltpu.VMEM((tm, d), BF16)],
        compiler_params=_compiler_params(("parallel", "parallel")),
        name="inproj0",
    )(x, norm_g.reshape(1, d), scale, shift, w_in, cos, sin)


def _ctx_kv_kernel(c_ref, g_ref, scale_ref, shift_ref, wk_ref, wv_ref, k_ref, v_ref):
    h = _norm_mod(c_ref[0], g_ref[...], scale_ref[...], shift_ref[...]).astype(BF16)
    k_ref[0] = jnp.dot(h, wk_ref[...], preferred_element_type=F32).astype(k_ref.dtype)
    v_ref[0] = jnp.dot(h, wv_ref[...], preferred_element_type=F32).astype(v_ref.dtype)


def _ctx_kv(ctx, norm_g, scale_c, shift_c, w_in):
    b, c, d = ctx.shape
    vec = pl.BlockSpec((1, d), lambda bi: (0, 0))
    kblk = ATTN_WIDTH // KV_WIDTH
    out = pl.BlockSpec((1, c, KV_WIDTH), lambda bi: (bi, 0, 0))
    return pl.pallas_call(
        _ctx_kv_kernel,
        grid=(b,),
        in_specs=[pl.BlockSpec((1, c, d), lambda bi: (bi, 0, 0)), vec, vec, vec,
                  pl.BlockSpec((d, KV_WIDTH), lambda bi: (0, kblk)),
                  pl.BlockSpec((d, KV_WIDTH), lambda bi: (0, kblk + 1))],
        out_specs=[out, out],
        out_shape=[jax.ShapeDtypeStruct((b, c, KV_WIDTH), BF16)] * 2,
        compiler_params=_compiler_params(("arbitrary",)),
        name="ctx_kv",
    )(ctx, norm_g.reshape(1, d), scale_c, shift_c, w_in, w_in)


ATTN_TQ = 1024
Q_ROWS = GQA_GROUP * WINDOW


def _attn_kernel(sink_ref, q_ref, k_ref, kp_ref, kn_ref, v_ref, vp_ref, vn_ref, kc_ref, vc_ref,
                 o_ref, kext, vext):
    i = pl.program_id(1)
    nblk_total = pl.num_programs(1) * (ATTN_TQ // WINDOW)
    kext[0:WINDOW] = kp_ref[0]
    kext[WINDOW:WINDOW + ATTN_TQ] = k_ref[0]
    kext[WINDOW + ATTN_TQ:] = kn_ref[0]
    vext[0:WINDOW] = vp_ref[0]
    vext[WINDOW:WINDOW + ATTN_TQ] = v_ref[0]
    vext[WINDOW + ATTN_TQ:] = vn_ref[0]

    r = lax.broadcasted_iota(jnp.int32, (Q_ROWS, WINDOW), 0) % WINDOW
    c = lax.broadcasted_iota(jnp.int32, (Q_ROWS, WINDOW), 1)
    neg = jnp.float32(-jnp.inf)
    bias_prev = jnp.where(c >= r, 0.0, neg).astype(F32)
    bias_next = jnp.where(c <= r, 0.0, neg).astype(F32)

    def block(j, carry):
        base = pl.multiple_of(j * WINDOW, WINDOW)
        n = i * (ATTN_TQ // WINDOW) + j
        bp = jnp.where(n > 0, bias_prev, neg)
        bn = jnp.where(n < nblk_total - 1, bias_next, neg)
        for kh in range(N_KV_HEADS):
            cols = slice(kh * HEAD_DIM, (kh + 1) * HEAD_DIM)
            q3 = jnp.concatenate(
                [q_ref[0, pl.ds(base, WINDOW), (kh * GQA_GROUP + g) * HEAD_DIM:(kh * GQA_GROUP + g + 1) * HEAD_DIM]
                 for g in range(GQA_GROUP)], axis=0)
            kw = kext[pl.ds(base, 3 * WINDOW), cols]
            vw = vext[pl.ds(base, 3 * WINDOW), cols]
            nt = (((1,), (1,)), ((), ()))
            s_win = lax.dot_general(q3, kw, nt, preferred_element_type=F32)
            s_ctx = lax.dot_general(q3, kc_ref[0, :, cols], nt, preferred_element_type=F32)
            sp = s_win[:, 0:WINDOW] + bp
            ss = s_win[:, WINDOW:2 * WINDOW]
            sn = s_win[:, 2 * WINDOW:] + bn
            sink = jnp.concatenate(
                [jnp.full((WINDOW, 1), sink_ref[kh * GQA_GROUP + g], F32) for g in range(GQA_GROUP)], axis=0)
            mx = jnp.maximum(jnp.maximum(sp, ss), sn)
            n_ctx = s_ctx.shape[1] // LANES
            for t in range(n_ctx):
                mx = jnp.maximum(mx, s_ctx[:, t * LANES:(t + 1) * LANES])
            m = jnp.maximum(jnp.max(mx, axis=1, keepdims=True), sink)
            pp, ps, pn = jnp.exp(sp - m), jnp.exp(ss - m), jnp.exp(sn - m)
            pc = jnp.exp(s_ctx - m)
            lsum = pp + ps + pn
            for t in range(n_ctx):
                lsum = lsum + pc[:, t * LANES:(t + 1) * LANES]
            denom = jnp.sum(lsum, axis=1, keepdims=True) + jnp.exp(sink - m)
            p_win = jnp.concatenate([pp, ps, pn], axis=1).astype(BF16)
            o = jnp.dot(p_win, vw, preferred_element_type=F32)
            o = o + jnp.dot(pc.astype(BF16), vc_ref[0, :, cols], preferred_element_type=F32)
            o = o * (1.0 / denom)
            for g in range(GQA_GROUP):
                h = kh * GQA_GROUP + g
                o_ref[0, pl.ds(base, WINDOW), h * HEAD_DIM:(h + 1) * HEAD_DIM] = (
                    o[g * WINDOW:(g + 1) * WINDOW].astype(o_ref.dtype))
        return carry

    lax.fori_loop(0, ATTN_TQ // WINDOW, block, 0)


def _attention(q, k, v, kc, vc, sink):
    b, s, _ = q.shape
    c = kc.shape[1]
    steps = s // ATTN_TQ
    per = ATTN_TQ // WINDOW
    nblk = s // WINDOW
    main = lambda width: pl.BlockSpec((1, ATTN_TQ, width), lambda bi, i: (bi, i, 0))
    prev = pl.BlockSpec((1, WINDOW, KV_WIDTH), lambda bi, i: (bi, jnp.maximum(i * per - 1, 0), 0))
    nxt = pl.BlockSpec((1, WINDOW, KV_WIDTH), lambda bi, i: (bi, jnp.minimum((i + 1) * per, nblk - 1), 0))
    ctx = pl.BlockSpec((1, c, KV_WIDTH), lambda bi, i: (bi, 0, 0))
    return pl.pallas_call(
        _attn_kernel,
        grid=(b, steps),
        in_specs=[pl.BlockSpec(memory_space=pltpu.SMEM), main(ATTN_WIDTH),
                  main(KV_WIDTH), prev, nxt, main(KV_WIDTH), prev, nxt, ctx, ctx],
        out_specs=main(ATTN_WIDTH),
        out_shape=jax.ShapeDtypeStruct((b, s, ATTN_WIDTH), BF16),
        scratch_shapes=[pltpu.VMEM((ATTN_TQ + 2 * WINDOW, KV_WIDTH), BF16)] * 2,
        compiler_params=_compiler_params(("parallel", "parallel")),
        name="window_attn",
    )(sink, q, k, k, k, v, v, v, kc, vc)


FFT_STEP = SUBLANES


def _fourier_kernel(x_ref, s1_ref, chan_ref, wf_ref, s2_ref, o_ref, hr_scr, hi_scr, w_scr):
    r = FFT_RADIX
    phase = pl.program_id(2)
    t = pl.program_id(3)

    @pl.when((phase == 0) & (t == 0))
    def _():
        w = jnp.dot(chan_ref[...].astype(BF16), wf_ref[0].astype(BF16), preferred_element_type=F32)
        w_scr[:, 0:r] = w[0:r].astype(BF16)
        w_scr[:, r:2 * r] = w[r:2 * r].astype(BF16)

    @pl.when(phase == 0)
    def _():
        s1 = s1_ref[...].astype(BF16)
        for j in range(FFT_STEP):
            xm = x_ref[:, j, :].astype(BF16)
            cm = jnp.dot(s1, xm, preferred_element_type=F32).astype(BF16)
            y = jnp.dot(cm, w_scr[...], preferred_element_type=F32)
            row = pl.multiple_of((t * FFT_STEP + j) * FFT_PITCH, SUBLANES)
            hr_scr[pl.ds(row, r), :] = y[0:r, 0:r] - y[r:2 * r, r:2 * r]
            hi_scr[pl.ds(row, r), :] = y[0:r, r:2 * r] + y[r:2 * r, 0:r]

    @pl.when(phase == 1)
    def _():
        for j in range(FFT_STEP):
            v = t * FFT_STEP + j
            hr = hr_scr[pl.ds(v, r, stride=FFT_PITCH), :]
            hi = hi_scr[pl.ds(v, r, stride=FFT_PITCH), :]
            hcat = jnp.concatenate([hr, hi], axis=0).astype(BF16)
            z = jnp.dot(s2_ref[0, j].astype(BF16), hcat, preferred_element_type=F32)
            o_ref[:, j, :] = z


def _fourier_mix(f, w_f):
    b, s, _ = f.shape
    r = FFT_RADIX
    assert s == r * r
    steps = r // FFT_STEP
    s1, chan, s2 = _dft_tables()
    f5 = f.reshape(b, r, steps, FFT_STEP, FOURIER_WIDTH)
    s2 = s2.reshape(steps, FFT_STEP, r, 2 * r)
    last = steps - 1
    x_spec = pl.BlockSpec((None, r, None, FFT_STEP, LANES),
                          lambda bi, g, ph, t: (bi, 0, jnp.where(ph == 0, t, last), 0, g))
    o_spec = pl.BlockSpec((None, r, None, FFT_STEP, LANES),
                          lambda bi, g, ph, t: (bi, 0, jnp.where(ph == 1, t, 0), 0, g))
    const2 = lambda shape: pl.BlockSpec(shape, lambda bi, g, ph, t: (0, 0))
    z5 = pl.pallas_call(
        _fourier_kernel,
        grid=(b, FOURIER_GROUPS, 2, steps),
        in_specs=[x_spec, const2((2 * r, r)), const2((2 * r, r)),
                  pl.BlockSpec((1, r, r), lambda bi, g, ph, t: (g, 0, 0)),
                  pl.BlockSpec((1, FFT_STEP, r, 2 * r), lambda bi, g, ph, t: (jnp.where(ph == 1, t, 0), 0, 0, 0))],
        out_specs=o_spec,
        out_shape=jax.ShapeDtypeStruct((b, r, steps, FFT_STEP, FOURIER_WIDTH), F32),
        scratch_shapes=[pltpu.VMEM((r * FFT_PITCH, LANES), F32), pltpu.VMEM((r * FFT_PITCH, LANES), F32),
                        pltpu.VMEM((r, 2 * r), BF16)],
        compiler_params=_compiler_params(("parallel", "parallel", "arbitrary", "arbitrary")),
        name="fourier_mix",
    )(f5, jnp.asarray(s1), jnp.asarray(chan), w_f, jnp.asarray(s2))
    return z5.reshape(b, s, FOURIER_WIDTH)


def _outproj0_kernel(a_ref, zf_ref, sz_ref, x_ref, gate_ref, w_ref, o_ref):
    ya = a_ref[0] * sz_ref[0, :, 0:ATTN_WIDTH]
    yf = zf_ref[0].astype(BF16) * sz_ref[0, :, ATTN_WIDTH:]
    acc = jnp.dot(ya, w_ref[0:ATTN_WIDTH, :], preferred_element_type=F32)
    acc = acc + jnp.dot(yf, w_ref[ATTN_WIDTH:, :], preferred_element_type=F32)
    o_ref[0] = x_ref[0] + gate_ref[0] * acc


def _outproj0(attn, zf, sz, x, gate, w_out, tm=512):
    b, s, d = x.shape
    tok = lambda width: pl.BlockSpec((1, tm, width), lambda bi, i: (bi, i, 0))
    return pl.pallas_call(
        _outproj0_kernel,
        grid=(b, s // tm),
        in_specs=[tok(ATTN_WIDTH), tok(FOURIER_WIDTH), tok(d), tok(d),
                  pl.BlockSpec((1, 1, d), lambda bi, i: (bi, 0, 0)),
                  _resident((d, d), lambda bi, i: (0, 0))],
        out_specs=tok(d),
        out_shape=jax.ShapeDtypeStruct((b, s, d), F32),
        compiler_params=_compiler_params(("parallel", "parallel")),
        name="outproj0",
    )(attn, zf, sz, x, gate, w_out)


IN1_CHUNK = 512


def _inproj1_kernel(x_ref, g_ref, scale_ref, shift_ref, w_ref, u_ref, sz_ref, h_scr):
    h_scr[...] = _norm_mod(x_ref[0], g_ref[...], scale_ref[0], shift_ref[0]).astype(BF16)
    for c0 in range(0, 2 * D_MODEL, IN1_CHUNK):
        acc = jnp.dot(h_scr[...], w_ref[:, c0:c0 + IN1_CHUNK], preferred_element_type=F32)
        if c0 < D_MODEL:
            u_ref[0, :, c0:c0 + IN1_CHUNK] = acc.astype(u_ref.dtype)
        else:
            sz_ref[0, :, c0 - D_MODEL:c0 - D_MODEL + IN1_CHUNK] = _silu(acc).astype(sz_ref.dtype)


def _inproj1(x, norm_g, scale, shift, w_in, tm=512):
    b, s, d = x.shape
    vec = pl.BlockSpec((1, 1, d), lambda bi, i: (bi, 0, 0))
    tok = pl.BlockSpec((1, tm, d), lambda bi, i: (bi, i, 0))
    return pl.pallas_call(
        _inproj1_kernel,
        grid=(b, s // tm),
        in_specs=[tok, _resident((1, d), lambda bi, i: (0, 0)), vec, vec,
                  _resident((d, 2 * d), lambda bi, i: (0, 0))],
        out_specs=[tok, tok],
        out_shape=[jax.ShapeDtypeStruct((b, s, d), BF16)] * 2,
        scratch_shapes=[pltpu.VMEM((tm, d), BF16)],
        compiler_params=_compiler_params(("parallel", "parallel")),
        name="inproj1",
    )(x, norm_g.reshape(1, d), scale, shift, w_in)


POOL_HALO = BF16_SUBLANES


def _pool_out_kernel(u_ref, up_ref, un_ref, sz_ref, x_ref, gate_ref, ls_ref, fn_ref,
                     inv_first_ref, inv_last_ref, inv_mid_ref, wg_ref, wo_ref, o_ref, y_scr):
    i = pl.program_id(1)
    last = pl.num_programs(1) - 1
    tm = u_ref.shape[1]
    halo = POOL_HALO
    up = jnp.where(i > 0, up_ref[0].astype(F32), 0.0)
    un = jnp.where(i < last, un_ref[0].astype(F32), 0.0)
    rows = tm + 2 * halo
    for g, w in enumerate(POOL_WINDOWS):
        cols = slice(g * POOL_GROUP_DIM, (g + 1) * POOL_GROUP_DIM)
        uc = u_ref[0, :, cols].astype(F32)
        a = jnp.concatenate([up[:, cols], uc, un[:, cols]], axis=0)
        span = 1
        while span < w:
            a = a + pltpu.roll(a, span, 0)
            span *= 2
        if w // 2 > 1:
            a = pltpu.roll(a, rows - (w // 2 - 1), 0)
        wsum = a[halo:halo + tm]
        inv_mid = inv_mid_ref[:, cols]
        inv_head = jnp.where(i == 0, inv_first_ref[:, cols], inv_mid)
        inv_tail = jnp.where(i == last, inv_last_ref[:, cols], inv_mid)
        mean = jnp.concatenate([wsum[0:SUBLANES] * inv_head,
                                wsum[SUBLANES:tm - SUBLANES] * inv_mid,
                                wsum[tm - SUBLANES:] * inv_tail], axis=0)
        pooled = (mean - uc).astype(BF16)
        y_scr[:, cols] = jnp.dot(pooled, wg_ref[g], preferred_element_type=F32)

    yz = (y_scr[...] * ls_ref[...]).astype(BF16) * sz_ref[0]
    acc = jnp.dot(yz, wo_ref[...], preferred_element_type=F32)
    x2 = x_ref[0] + gate_ref[0] * acc
    ms = jnp.mean(x2 * x2, axis=-1, keepdims=True)
    o_ref[0] = x2 * lax.rsqrt(ms + RMS_EPS) * fn_ref[...]


def _pool_out(u, sz, x, gate, layer_scale, final_norm, w_grp, w_out, tm=512):
    b, s, d = x.shape
    per = tm // POOL_HALO
    nh = s // POOL_HALO
    inv_first, inv_last, inv_mid = _pool_edge_tables(s)
    tok = pl.BlockSpec((1, tm, d), lambda bi, i: (bi, i, 0))
    prev = pl.BlockSpec((1, POOL_HALO, d), lambda bi, i: (bi, jnp.maximum(i * per - 1, 0), 0))
    nxt = pl.BlockSpec((1, POOL_HALO, d), lambda bi, i: (bi, jnp.minimum((i + 1) * per, nh - 1), 0))
    row = lambda rows: _resident((rows, d), lambda bi, i: (0, 0))
    return pl.pallas_call(
        _pool_out_kernel,
        grid=(b, s // tm),
        in_specs=[tok, prev, nxt, tok, tok, pl.BlockSpec((1, 1, d), lambda bi, i: (bi, 0, 0)),
                  row(1), row(1), row(SUBLANES), row(SUBLANES), row(1),
                  _resident((POOL_GROUPS, POOL_GROUP_DIM, POOL_GROUP_DIM), lambda bi, i: (0, 0, 0)),
                  _resident((d, d), lambda bi, i: (0, 0))],
        out_specs=tok,
        out_shape=jax.ShapeDtypeStruct((b, s, d), F32),
        scratch_shapes=[pltpu.VMEM((tm, d), F32)],
        compiler_params=_compiler_params(("parallel", "parallel")),
        name="pool_out",
    )(u, u, u, sz, x, gate, layer_scale.reshape(1, d), final_norm.reshape(1, d),
      jnp.asarray(inv_first), jnp.asarray(inv_last), jnp.asarray(inv_mid), w_grp, w_out)


def kernel(x, c, ctx, c_ctx, l0_norm, l0_w_mod, l0_b_mod, l0_w_in, l0_sink, l0_w_f, l0_w_out,
           l1_norm, l1_w_mod, l1_b_mod, l1_w_in, l1_w_grp, l1_scale, l1_w_out, final_norm):
    b, s, d = x.shape
    cond = jnp.concatenate([c, c_ctx[None, :], jnp.zeros((COND_ROWS - b - 1, d), F32)], axis=0)
    m0 = _modulation(cond, l0_w_mod, l0_b_mod)
    m1 = _modulation(cond, l1_w_mod, l1_b_mod)
    tok_vec = lambda m, k: m[:b, k * d:(k + 1) * d].reshape(b, 1, d)
    shift0, scale0, gate0 = (tok_vec(m0, k) for k in range(3))
    shift1, scale1, gate1 = (tok_vec(m1, k) for k in range(3))
    shift_c, scale_c = m0[b:b + 1, 0:d], m0[b:b + 1, d:2 * d]

    w_in0 = l0_w_in.astype(BF16)
    cos, sin = _rope_tables(s)
    q, k, v, f, sz0 = _inproj0(x, l0_norm, scale0, shift0, w_in0, jnp.asarray(cos), jnp.asarray(sin))
    kc, vc = _ctx_kv(ctx, l0_norm, scale_c, shift_c, w_in0)
    attn = _attention(q, k, v, kc, vc, l0_sink)
    zf = _fourier_mix(f, l0_w_f)
    x1 = _outproj0(attn, zf, sz0, x, gate0, l0_w_out.astype(BF16))

    u, sz1 = _inproj1(x1, l1_norm, scale1, shift1, l1_w_in.astype(BF16))
    return _pool_out(u, sz1, x1, gate1, l1_scale, final_norm, l1_w_grp.astype(BF16), l1_w_out.astype(BF16))
```

```python
import functools
import math

import numpy as np
import jax
import jax.numpy as jnp
from jax import lax
from jax.experimental import pallas as pl
from jax.experimental.pallas import tpu as pltpu

F32 = jnp.float32
BF16 = jnp.bfloat16

D_MODEL = 2048
GRID_W = 64
HEAD_DIM = 128
N_Q_HEADS = 12
N_KV_HEADS = 4
GQA_GROUP = N_Q_HEADS // N_KV_HEADS
WINDOW = 128
ATTN_WIDTH = N_Q_HEADS * HEAD_DIM
KV_WIDTH = N_KV_HEADS * HEAD_DIM
FOURIER_WIDTH = D_MODEL - ATTN_WIDTH
FOURIER_GROUPS = 4
FOURIER_GROUP_DIM = FOURIER_WIDTH // FOURIER_GROUPS
IN0_WIDTH = ATTN_WIDTH + 2 * KV_WIDTH + FOURIER_WIDTH + D_MODEL
POOL_WINDOWS = (2, 4, 8, 16)
POOL_GROUPS = len(POOL_WINDOWS)
POOL_GROUP_DIM = D_MODEL // POOL_GROUPS
ROPE_BASE = 10000.0
ROPE_PAIRS = HEAD_DIM // 4
RMS_EPS = 1e-6

LANES = 128
SUBLANES = 8
BF16_SUBLANES = 16
VMEM_LIMIT_BYTES = 56 * 1024 * 1024

FFT_RADIX = 128
FFT_PITCH = FFT_RADIX + SUBLANES

COND_ROWS = 8


def _silu(x):
    return x * jax.nn.sigmoid(x)


def _compiler_params(semantics):
    return pltpu.CompilerParams(dimension_semantics=semantics, vmem_limit_bytes=VMEM_LIMIT_BYTES)


def _resident(block_shape, index_map):
    return pl.BlockSpec(block_shape, index_map, pipeline_mode=pl.Buffered(1))


@functools.lru_cache(maxsize=None)
def _rope_tables(seq):
    pos = np.arange(seq)
    row = (pos // GRID_W).astype(np.float64)
    col = (pos % GRID_W).astype(np.float64)
    inv = 1.0 / (ROPE_BASE ** (np.arange(ROPE_PAIRS, dtype=np.float64) / ROPE_PAIRS))
    ar, ac = row[:, None] * inv, col[:, None] * inv
    cos = np.concatenate([np.cos(ar), np.cos(ar), np.cos(ac), np.cos(ac)], axis=1)
    sin = np.concatenate([-np.sin(ar), np.sin(ar), -np.sin(ac), np.sin(ac)], axis=1)
    return cos.astype(np.float32), sin.astype(np.float32)


@functools.lru_cache(maxsize=None)
def _dft_tables():
    r = FFT_RADIX
    idx = np.arange(r)
    ang = 2.0 * np.pi * np.outer(idx, idx) / r
    c, s = np.cos(ang) / math.sqrt(r), np.sin(ang) / math.sqrt(r)
    stage1 = np.concatenate([c, -s], axis=0)
    chan = np.concatenate([c, -s], axis=0)
    v = idx[:, None, None]
    u = idx[None, :, None]
    n1 = idx[None, None, :]
    k = (n1 * (r * u + v)) % (r * r)
    ang2 = 2.0 * np.pi * k / (r * r)
    stage2 = np.concatenate([np.cos(ang2), np.sin(ang2)], axis=2) / math.sqrt(r)
    return stage1.astype(np.float32), chan.astype(np.float32), stage2.astype(np.float32)


@functools.lru_cache(maxsize=None)
def _pool_edge_tables(seq):
    first = np.zeros((SUBLANES, D_MODEL), np.float64)
    last = np.zeros((SUBLANES, D_MODEL), np.float64)
    for g, w in enumerate(POOL_WINDOWS):
        for r in range(SUBLANES):
            for tab, t in ((first, r), (last, seq - SUBLANES + r)):
                lo = min(max(t - w // 2, 0), seq)
                hi = min(max(t - w // 2 + w, 0), seq)
                tab[r, g * POOL_GROUP_DIM:(g + 1) * POOL_GROUP_DIM] = 1.0 / (hi - lo)
    interior = np.repeat(1.0 / np.array(POOL_WINDOWS, np.float64), POOL_GROUP_DIM)[None, :]
    return first.astype(np.float32), last.astype(np.float32), interior.astype(np.float32)


def _mod_kernel(cond_ref, w_ref, b_ref, o_ref):
    a = _silu(cond_ref[...]).astype(BF16)
    o_ref[...] = jnp.dot(a, w_ref[...].astype(BF16), preferred_element_type=F32) + b_ref[...]


def _modulation(cond, w_mod, b_mod):
    d, n = w_mod.shape
    tn = 1536
    return pl.pallas_call(
        _mod_kernel,
        grid=(n // tn,),
        in_specs=[
            pl.BlockSpec((COND_ROWS, d), lambda j: (0, 0)),
            pl.BlockSpec((d, tn), lambda j: (0, j)),
            pl.BlockSpec((1, tn), lambda j: (0, j)),
        ],
        out_specs=pl.BlockSpec((COND_ROWS, tn), lambda j: (0, j)),
        out_shape=jax.ShapeDtypeStruct((COND_ROWS, n), F32),
        compiler_params=_compiler_params(("arbitrary",)),
        name="modulation",
    )(cond, w_mod, b_mod.reshape(1, n))


def _norm_mod(x, g, scale, shift):
    ms = jnp.mean(x * x, axis=-1, keepdims=True)
    return x * lax.rsqrt(ms + RMS_EPS) * g * (1.0 + scale) + shift


IN0_CHUNK = 512


def _rope(x, cos, sin, partner_is_right):
    rot = jnp.where(partner_is_right, pltpu.roll(x, HEAD_DIM - ROPE_PAIRS, 1), pltpu.roll(x, ROPE_PAIRS, 1))
    return x * cos + rot * sin


def _inproj0_kernel(x_ref, g_ref, scale_ref, shift_ref, w_ref, cos_ref, sin_ref,
                    q_ref, k_ref, v_ref, f_ref, sz_ref, h_scr):
    h_scr[...] = _norm_mod(x_ref[0], g_ref[...], scale_ref[0], shift_ref[0]).astype(BF16)
    cos = cos_ref[...]
    sin = sin_ref[...]
    q_scale = HEAD_DIM ** -0.5 * math.log2(math.e)
    cos_q, sin_q = cos * q_scale, sin * q_scale
    lane = lax.broadcasted_iota(jnp.int32, cos.shape, 1)
    partner_is_right = (lane % (2 * ROPE_PAIRS)) < ROPE_PAIRS
    heads_per_chunk = IN0_CHUNK // HEAD_DIM

    for c0 in range(0, IN0_WIDTH, IN0_CHUNK):
        acc = jnp.dot(h_scr[...], w_ref[:, c0:c0 + IN0_CHUNK], preferred_element_type=F32)
        if c0 < ATTN_WIDTH + KV_WIDTH:
            is_q = c0 < ATTN_WIDTH
            dst, off = (q_ref, c0) if is_q else (k_ref, c0 - ATTN_WIDTH)
            cs, sn = (cos_q, sin_q) if is_q else (cos, sin)
            for j in range(heads_per_chunk):
                xh = acc[:, j * HEAD_DIM:(j + 1) * HEAD_DIM]
                lo = off + j * HEAD_DIM
                dst[0, :, lo:lo + HEAD_DIM] = _rope(xh, cs, sn, partner_is_right).astype(dst.dtype)
        elif c0 < ATTN_WIDTH + 2 * KV_WIDTH:
            v_ref[0] = acc.astype(v_ref.dtype)
        elif c0 < ATTN_WIDTH + 2 * KV_WIDTH + FOURIER_WIDTH:
            f_ref[0] = acc
        else:
            off = c0 - (IN0_WIDTH - D_MODEL)
            sz_ref[0, :, off:off + IN0_CHUNK] = _silu(acc).astype(sz_ref.dtype)


def _inproj0(x, norm_g, scale, shift, w_in, cos, sin, tm=512):
    b, s, d = x.shape
    vec = pl.BlockSpec((1, 1, d), lambda bi, i: (bi, 0, 0))
    tok = lambda width: pl.BlockSpec((1, tm, width), lambda bi, i: (bi, i, 0))
    tab = pl.BlockSpec((tm, HEAD_DIM), lambda bi, i: (i, 0))
    return pl.pallas_call(
        _inproj0_kernel,
        grid=(b, s // tm),
        in_specs=[tok(d), _resident((1, d), lambda bi, i: (0, 0)), vec, vec,
                  _resident((d, IN0_WIDTH), lambda bi, i: (0, 0)), tab, tab],
        out_specs=[tok(ATTN_WIDTH), tok(KV_WIDTH), tok(KV_WIDTH), tok(FOURIER_WIDTH), tok(D_MODEL)],
        out_shape=[
            jax.ShapeDtypeStruct((b, s, ATTN_WIDTH), BF16),
            jax.ShapeDtypeStruct((b, s, KV_WIDTH), BF16),
            jax.ShapeDtypeStruct((b, s, KV_WIDTH), BF16),
            jax.ShapeDtypeStruct((b, s, FOURIER_WIDTH), F32),
            jax.ShapeDtypeStruct((b, s, D_MODEL), BF16),
        ],
        scratch_shapes=[pltpu.VMEM((tm, d), BF16)],
        compiler_params=_compiler_params(("parallel", "parallel")),
        name="inproj0",
    )(x, norm_g.reshape(1, d), scale, shift, w_in, cos, sin)


def _ctx_kv_kernel(c_ref, g_ref, scale_ref, shift_ref, wk_ref, wv_ref, k_ref, v_ref):
    h = _norm_mod(c_ref[0], g_ref[...], scale_ref[...], shift_ref[...]).astype(BF16)
    k_ref[0] = jnp.dot(h, wk_ref[...], preferred_element_type=F32).astype(k_ref.dtype)
    v_ref[0] = jnp.dot(h, wv_ref[...], preferred_element_type=F32).astype(v_ref.dtype)


def _ctx_kv(ctx, norm_g, scale_c, shift_c, w_in):
    b, c, d = ctx.shape
    vec = pl.BlockSpec((1, d), lambda bi: (0, 0))
    kblk = ATTN_WIDTH // KV_WIDTH
    out = pl.BlockSpec((1, c, KV_WIDTH), lambda bi: (bi, 0, 0))
    return pl.pallas_call(
        _ctx_kv_kernel,
        grid=(b,),
        in_specs=[pl.BlockSpec((1, c, d), lambda bi: (bi, 0, 0)), vec, vec, vec,
                  pl.BlockSpec((d, KV_WIDTH), lambda bi: (0, kblk)),
                  pl.BlockSpec((d, KV_WIDTH), lambda bi: (0, kblk + 1))],
        out_specs=[out, out],
        out_shape=[jax.ShapeDtypeStruct((b, c, KV_WIDTH), BF16)] * 2,
        compiler_params=_compiler_params(("arbitrary",)),
        name="ctx_kv",
    )(ctx, norm_g.reshape(1, d), scale_c, shift_c, w_in, w_in)


ATTN_TQ = 1024
Q_ROWS = GQA_GROUP * WINDOW


def _head_cols(h):
    return slice(h * HEAD_DIM, (h + 1) * HEAD_DIM)


def _attn_kernel(sink_ref, q_ref, k_ref, kp_ref, kn_ref, v_ref, vp_ref, vn_ref, kc_ref, vc_ref,
                 o_ref, kext, vaug, vcaug, s0, s1, p0, p1, d0, d1):
    i = pl.program_id(1)
    nb = ATTN_TQ // WINDOW
    nblk_total = pl.num_programs(1) * nb
    n_ctx = kc_ref.shape[1]
    s_scr, p_scr, d_scr = (s0, s1), (p0, p1), (d0, d1)

    kext[0:WINDOW] = kp_ref[0]
    kext[WINDOW:WINDOW + ATTN_TQ] = k_ref[0]
    kext[WINDOW + ATTN_TQ:] = kn_ref[0]
    ones = jnp.ones((ATTN_TQ + 2 * WINDOW, HEAD_DIM), BF16)
    for kh in range(N_KV_HEADS):
        vaug[kh, 0:WINDOW, 0:HEAD_DIM] = vp_ref[0, :, _head_cols(kh)]
        vaug[kh, WINDOW:WINDOW + ATTN_TQ, 0:HEAD_DIM] = v_ref[0, :, _head_cols(kh)]
        vaug[kh, WINDOW + ATTN_TQ:, 0:HEAD_DIM] = vn_ref[0, :, _head_cols(kh)]
        vaug[kh, :, HEAD_DIM:] = ones
        vcaug[kh, :, 0:HEAD_DIM] = vc_ref[0, :, _head_cols(kh)]
        vcaug[kh, :, HEAD_DIM:] = ones[0:n_ctx]

    r = lax.broadcasted_iota(jnp.int32, (Q_ROWS, WINDOW), 0) % WINDOW
    c = lax.broadcasted_iota(jnp.int32, (Q_ROWS, WINDOW), 1)
    neg = jnp.float32(-jnp.inf)
    bias_prev = jnp.where(c >= r, 0.0, neg).astype(F32)
    bias_next = jnp.where(c <= r, 0.0, neg).astype(F32)
    nt = (((1,), (1,)), ((), ()))

    def scores(j, kh, s_ref):
        base = pl.multiple_of(j * WINDOW, WINDOW)
        n = i * nb + j
        q3 = jnp.concatenate([q_ref[0, pl.ds(base, WINDOW), _head_cols(kh * GQA_GROUP + g)]
                              for g in range(GQA_GROUP)], axis=0)
        s_ctx = lax.dot_general(q3, kc_ref[0, :, _head_cols(kh)], nt, preferred_element_type=F32)
        s_win = lax.dot_general(q3, kext[pl.ds(base, 3 * WINDOW), _head_cols(kh)], nt,
                                preferred_element_type=F32)
        s_ref[kh, :, 0:n_ctx] = s_ctx
        s_ref[kh, :, n_ctx:n_ctx + WINDOW] = s_win[:, 0:WINDOW] + jnp.where(n > 0, bias_prev, neg)
        s_ref[kh, :, n_ctx + WINDOW:n_ctx + 2 * WINDOW] = s_win[:, WINDOW:2 * WINDOW]
        s_ref[kh, :, n_ctx + 2 * WINDOW:] = s_win[:, 2 * WINDOW:] + jnp.where(n < nblk_total - 1, bias_next, neg)

    def softmax(kh, s_ref, p_ref, d_ref):
        ntile = s_ref.shape[2] // LANES
        tiles = [s_ref[kh, :, t * LANES:(t + 1) * LANES] for t in range(ntile)]
        mx = tiles[0]
        for t in range(1, ntile):
            mx = jnp.maximum(mx, tiles[t])
        sink = jnp.concatenate([jnp.full((WINDOW, 1), sink_ref[kh * GQA_GROUP + g] * math.log2(math.e), F32)
                                for g in range(GQA_GROUP)], axis=0)
        m = jnp.maximum(jnp.max(mx, axis=1, keepdims=True), sink)
        for t in range(ntile):
            p_ref[kh, :, t * LANES:(t + 1) * LANES] = jnp.exp2(tiles[t] - m).astype(BF16)
        d_ref[kh] = jnp.broadcast_to(jnp.exp2(sink - m), (Q_ROWS, HEAD_DIM))

    def values(j, kh, p_ref, d_ref):
        base = pl.multiple_of(j * WINDOW, WINDOW)
        o = jnp.dot(p_ref[kh, :, 0:n_ctx], vcaug[kh], preferred_element_type=F32)
        o = o + jnp.dot(p_ref[kh, :, n_ctx:], vaug[kh, pl.ds(base, 3 * WINDOW), :], preferred_element_type=F32)
        res = o[:, 0:HEAD_DIM] * (1.0 / (o[:, HEAD_DIM:] + d_ref[kh]))
        for g in range(GQA_GROUP):
            o_ref[0, pl.ds(base, WINDOW), _head_cols(kh * GQA_GROUP + g)] = (
                res[g * WINDOW:(g + 1) * WINDOW].astype(o_ref.dtype))

    def tick(t, do_scores=True, do_softmax=True, do_values=True, parity=None):
        par = t % 2 if parity is None else parity
        for kh in range(N_KV_HEADS):
            if do_scores:
                scores(t, kh, s_scr[par])
            if do_softmax:
                softmax(kh, s_scr[1 - par], p_scr[1 - par], d_scr[1 - par])
            if do_values:
                values(t - 2, kh, p_scr[par], d_scr[par])

    tick(0, do_softmax=False, do_values=False)
    tick(1, do_values=False)

    def pair(jj, carry):
        tick(2 + 2 * jj, parity=0)
        tick(3 + 2 * jj, parity=1)
        return carry

    lax.fori_loop(0, (nb - 2) // 2, pair, 0)
    tick(nb, do_scores=False)
    tick(nb + 1, do_scores=False, do_softmax=False)


def _attention(q, k, v, kc, vc, sink):
    b, s, _ = q.shape
    c = kc.shape[1]
    steps = s // ATTN_TQ
    per = ATTN_TQ // WINDOW
    nblk = s // WINDOW
    main = lambda width: pl.BlockSpec((1, ATTN_TQ, width), lambda bi, i: (bi, i, 0))
    prev = pl.BlockSpec((1, WINDOW, KV_WIDTH), lambda bi, i: (bi, jnp.maximum(i * per - 1, 0), 0))
    nxt = pl.BlockSpec((1, WINDOW, KV_WIDTH), lambda bi, i: (bi, jnp.minimum((i + 1) * per, nblk - 1), 0))
    ctx = pl.BlockSpec((1, c, KV_WIDTH), lambda bi, i: (bi, 0, 0))
    return pl.pallas_call(
        _attn_kernel,
        grid=(b, steps),
        in_specs=[pl.BlockSpec(memory_space=pltpu.SMEM), main(ATTN_WIDTH),
                  main(KV_WIDTH), prev, nxt, main(KV_WIDTH), prev, nxt, ctx, ctx],
        out_specs=main(ATTN_WIDTH),
        out_shape=jax.ShapeDtypeStruct((b, s, ATTN_WIDTH), BF16),
        scratch_shapes=[pltpu.VMEM((ATTN_TQ + 2 * WINDOW, KV_WIDTH), BF16),
                        pltpu.VMEM((N_KV_HEADS, ATTN_TQ + 2 * WINDOW, 2 * HEAD_DIM), BF16),
                        pltpu.VMEM((N_KV_HEADS, c, 2 * HEAD_DIM), BF16)]
                       + [pltpu.VMEM((N_KV_HEADS, Q_ROWS, c + 3 * WINDOW), F32)] * 2
                       + [pltpu.VMEM((N_KV_HEADS, Q_ROWS, c + 3 * WINDOW), BF16)] * 2
                       + [pltpu.VMEM((N_KV_HEADS, Q_ROWS, HEAD_DIM), F32)] * 2,
        compiler_params=_compiler_params(("parallel", "parallel")),
        name="window_attn",
    )(sink, q, k, k, k, v, v, v, kc, vc)


FFT_STEP = SUBLANES


def _fourier_kernel(x_ref, s1_ref, chan_ref, wf_ref, s2_ref, o_ref, hr_scr, hi_scr, w_scr):
    r = FFT_RADIX
    phase = pl.program_id(2)
    t = pl.program_id(3)

    @pl.when((phase == 0) & (t == 0))
    def _():
        w = jnp.dot(chan_ref[...].astype(BF16), wf_ref[0].astype(BF16), preferred_element_type=F32)
        w_scr[:, 0:r] = w[0:r].astype(BF16)
        w_scr[:, r:2 * r] = w[r:2 * r].astype(BF16)

    @pl.when(phase == 0)
    def _():
        s1 = s1_ref[...].astype(BF16)
        for j in range(FFT_STEP):
            xm = x_ref[:, j, :].astype(BF16)
            cm = jnp.dot(s1, xm, preferred_element_type=F32).astype(BF16)
            y = jnp.dot(cm, w_scr[...], preferred_element_type=F32)
            row = pl.multiple_of((t * FFT_STEP + j) * FFT_PITCH, SUBLANES)
            hr_scr[pl.ds(row, r), :] = y[0:r, 0:r] - y[r:2 * r, r:2 * r]
            hi_scr[pl.ds(row, r), :] = y[0:r, r:2 * r] + y[r:2 * r, 0:r]

    @pl.when(phase == 1)
    def _():
        for j in range(FFT_STEP):
            v = t * FFT_STEP + j
            hr = hr_scr[pl.ds(v, r, stride=FFT_PITCH), :]
            hi = hi_scr[pl.ds(v, r, stride=FFT_PITCH), :]
            hcat = jnp.concatenate([hr, hi], axis=0).astype(BF16)
            z = jnp.dot(s2_ref[0, j].astype(BF16), hcat, preferred_element_type=F32)
            o_ref[:, j, :] = z


def _fourier_mix(f, w_f):
    b, s, _ = f.shape
    r = FFT_RADIX
    assert s == r * r
    steps = r // FFT_STEP
    s1, chan, s2 = _dft_tables()
    f5 = f.reshape(b, r, steps, FFT_STEP, FOURIER_WIDTH)
    s2 = s2.reshape(steps, FFT_STEP, r, 2 * r)
    last = steps - 1
    x_spec = pl.BlockSpec((None, r, None, FFT_STEP, LANES),
                          lambda bi, g, ph, t: (bi, 0, jnp.where(ph == 0, t, last), 0, g))
    o_spec = pl.BlockSpec((None, r, None, FFT_STEP, LANES),
                          lambda bi, g, ph, t: (bi, 0, jnp.where(ph == 1, t, 0), 0, g))
    const2 = lambda shape: pl.BlockSpec(shape, lambda bi, g, ph, t: (0, 0))
    z5 = pl.pallas_call(
        _fourier_kernel,
        grid=(b, FOURIER_GROUPS, 2, steps),
        in_specs=[x_spec, const2((2 * r, r)), const2((2 * r, r)),
                  pl.BlockSpec((1, r, r), lambda bi, g, ph, t: (g, 0, 0)),
                  pl.BlockSpec((1, FFT_STEP, r, 2 * r), lambda bi, g, ph, t: (jnp.where(ph == 1, t, 0), 0, 0, 0))],
        out_specs=o_spec,
        out_shape=jax.ShapeDtypeStruct((b, r, steps, FFT_STEP, FOURIER_WIDTH), F32),
        scratch_shapes=[pltpu.VMEM((r * FFT_PITCH, LANES), F32), pltpu.VMEM((r * FFT_PITCH, LANES), F32),
                        pltpu.VMEM((r, 2 * r), BF16)],
        compiler_params=_compiler_params(("parallel", "parallel", "arbitrary", "arbitrary")),
        name="fourier_mix",
    )(f5, jnp.asarray(s1), jnp.asarray(chan), w_f, jnp.asarray(s2))
    return z5.reshape(b, s, FOURIER_WIDTH)


def _outproj0_kernel(a_ref, zf_ref, sz_ref, x_ref, gate_ref, w_ref, o_ref):
    ya = a_ref[0] * sz_ref[0, :, 0:ATTN_WIDTH]
    yf = zf_ref[0].astype(BF16) * sz_ref[0, :, ATTN_WIDTH:]
    acc = jnp.dot(ya, w_ref[0:ATTN_WIDTH, :], preferred_element_type=F32)
    acc = acc + jnp.dot(yf, w_ref[ATTN_WIDTH:, :], preferred_element_type=F32)
    o_ref[0] = x_ref[0] + gate_ref[0] * acc


def _outproj0(attn, zf, sz, x, gate, w_out, tm=512):
    b, s, d = x.shape
    tok = lambda width: pl.BlockSpec((1, tm, width), lambda bi, i: (bi, i, 0))
    return pl.pallas_call(
        _outproj0_kernel,
        grid=(b, s // tm),
        in_specs=[tok(ATTN_WIDTH), tok(FOURIER_WIDTH), tok(d), tok(d),
                  pl.BlockSpec((1, 1, d), lambda bi, i: (bi, 0, 0)),
                  _resident((d, d), lambda bi, i: (0, 0))],
        out_specs=tok(d),
        out_shape=jax.ShapeDtypeStruct((b, s, d), F32),
        compiler_params=_compiler_params(("parallel", "parallel")),
        name="outproj0",
    )(attn, zf, sz, x, gate, w_out)


IN1_CHUNK = 512


def _inproj1_kernel(x_ref, g_ref, scale_ref, shift_ref, w_ref, u_ref, sz_ref, h_scr):
    h_scr[...] = _norm_mod(x_ref[0], g_ref[...], scale_ref[0], shift_ref[0]).astype(BF16)
    for c0 in range(0, 2 * D_MODEL, IN1_CHUNK):
        acc = jnp.dot(h_scr[...], w_ref[:, c0:c0 + IN1_CHUNK], preferred_element_type=F32)
        if c0 < D_MODEL:
            u_ref[0, :, c0:c0 + IN1_CHUNK] = acc.astype(u_ref.dtype)
        else:
            sz_ref[0, :, c0 - D_MODEL:c0 - D_MODEL + IN1_CHUNK] = _silu(acc).astype(sz_ref.dtype)


def _inproj1(x, norm_g, scale, shift, w_in, tm=512):
    b, s, d = x.shape
    vec = pl.BlockSpec((1, 1, d), lambda bi, i: (bi, 0, 0))
    tok = pl.BlockSpec((1, tm, d), lambda bi, i: (bi, i, 0))
    return pl.pallas_call(
        _inproj1_kernel,
        grid=(b, s // tm),
        in_specs=[tok, _resident((1, d), lambda bi, i: (0, 0)), vec, vec,
                  _resident((d, 2 * d), lambda bi, i: (0, 0))],
        out_specs=[tok, tok],
        out_shape=[jax.ShapeDtypeStruct((b, s, d), BF16)] * 2,
        scratch_shapes=[pltpu.VMEM((tm, d), BF16)],
        compiler_params=_compiler_params(("parallel", "parallel")),
        name="inproj1",
    )(x, norm_g.reshape(1, d), scale, shift, w_in)


POOL_HALO = BF16_SUBLANES


def _pool_out_kernel(u_ref, up_ref, un_ref, sz_ref, x_ref, gate_ref, ls_ref, fn_ref,
                     inv_first_ref, inv_last_ref, inv_mid_ref, wg_ref, wo_ref, o_ref, y_scr):
    i = pl.program_id(1)
    last = pl.num_programs(1) - 1
    tm = u_ref.shape[1]
    halo = POOL_HALO
    up = jnp.where(i > 0, up_ref[0].astype(F32), 0.0)
    un = jnp.where(i < last, un_ref[0].astype(F32), 0.0)
    rows = tm + 2 * halo
    for g, w in enumerate(POOL_WINDOWS):
        cols = slice(g * POOL_GROUP_DIM, (g + 1) * POOL_GROUP_DIM)
        uc = u_ref[0, :, cols].astype(F32)
        a = jnp.concatenate([up[:, cols], uc, un[:, cols]], axis=0)
        span = 1
        while span < w:
            a = a + pltpu.roll(a, span, 0)
            span *= 2
        if w // 2 > 1:
            a = pltpu.roll(a, rows - (w // 2 - 1), 0)
        wsum = a[halo:halo + tm]
        inv_mid = inv_mid_ref[:, cols]
        inv_head = jnp.where(i == 0, inv_first_ref[:, cols], inv_mid)
        inv_tail = jnp.where(i == last, inv_last_ref[:, cols], inv_mid)
        mean = jnp.concatenate([wsum[0:SUBLANES] * inv_head,
                                wsum[SUBLANES:tm - SUBLANES] * inv_mid,
                                wsum[tm - SUBLANES:] * inv_tail], axis=0)
        pooled = (mean - uc).astype(BF16)
        y_scr[:, cols] = jnp.dot(pooled, wg_ref[g], preferred_element_type=F32)

    yz = (y_scr[...] * ls_ref[...]).astype(BF16) * sz_ref[0]
    acc = jnp.dot(yz, wo_ref[...], preferred_element_type=F32)
    x2 = x_ref[0] + gate_ref[0] * acc
    ms = jnp.mean(x2 * x2, axis=-1, keepdims=True)
    o_ref[0] = x2 * lax.rsqrt(ms + RMS_EPS) * fn_ref[...]


def _pool_out(u, sz, x, gate, layer_scale, final_norm, w_grp, w_out, tm=512):
    b, s, d = x.shape
    per = tm // POOL_HALO
    nh = s // POOL_HALO
    inv_first, inv_last, inv_mid = _pool_edge_tables(s)
    tok = pl.BlockSpec((1, tm, d), lambda bi, i: (bi, i, 0))
    prev = pl.BlockSpec((1, POOL_HALO, d), lambda bi, i: (bi, jnp.maximum(i * per - 1, 0), 0))
    nxt = pl.BlockSpec((1, POOL_HALO, d), lambda bi, i: (bi, jnp.minimum((i + 1) * per, nh - 1), 0))
    row = lambda rows: _resident((rows, d), lambda bi, i: (0, 0))
    return pl.pallas_call(
        _pool_out_kernel,
        grid=(b, s // tm),
        in_specs=[tok, prev, nxt, tok, tok, pl.BlockSpec((1, 1, d), lambda bi, i: (bi, 0, 0)),
                  row(1), row(1), row(SUBLANES), row(SUBLANES), row(1),
                  _resident((POOL_GROUPS, POOL_GROUP_DIM, POOL_GROUP_DIM), lambda bi, i: (0, 0, 0)),
                  _resident((d, d), lambda bi, i: (0, 0))],
        out_specs=tok,
        out_shape=jax.ShapeDtypeStruct((b, s, d), F32),
        scratch_shapes=[pltpu.VMEM((tm, d), F32)],
        compiler_params=_compiler_params(("parallel", "parallel")),
        name="pool_out",
    )(u, u, u, sz, x, gate, layer_scale.reshape(1, d), final_norm.reshape(1, d),
      jnp.asarray(inv_first), jnp.asarray(inv_last), jnp.asarray(inv_mid), w_grp, w_out)


def kernel(x, c, ctx, c_ctx, l0_norm, l0_w_mod, l0_b_mod, l0_w_in, l0_sink, l0_w_f, l0_w_out,
           l1_norm, l1_w_mod, l1_b_mod, l1_w_in, l1_w_grp, l1_scale, l1_w_out, final_norm):
    b, s, d = x.shape
    cond = jnp.concatenate([c, c_ctx[None, :], jnp.zeros((COND_ROWS - b - 1, d), F32)], axis=0)
    m0 = _modulation(cond, l0_w_mod, l0_b_mod)
    m1 = _modulation(cond, l1_w_mod, l1_b_mod)
    tok_vec = lambda m, k: m[:b, k * d:(k + 1) * d].reshape(b, 1, d)
    shift0, scale0, gate0 = (tok_vec(m0, k) for k in range(3))
    shift1, scale1, gate1 = (tok_vec(m1, k) for k in range(3))
    shift_c, scale_c = m0[b:b + 1, 0:d], m0[b:b + 1, d:2 * d]

    w_in0 = l0_w_in.astype(BF16)
    cos, sin = _rope_tables(s)
    q, k, v, f, sz0 = _inproj0(x, l0_norm, scale0, shift0, w_in0, jnp.asarray(cos), jnp.asarray(sin))
    kc, vc = _ctx_kv(ctx, l0_norm, scale_c, shift_c, w_in0)
    attn = _attention(q, k, v, kc, vc, l0_sink)
    zf = _fourier_mix(f, l0_w_f)
    x1 = _outproj0(attn, zf, sz0, x, gate0, l0_w_out.astype(BF16))

    u, sz1 = _inproj1(x1, l1_norm, scale1, shift1, l1_w_in.astype(BF16))
    return _pool_out(u, sz1, x1, gate1, l1_scale, final_norm, l1_w_grp.astype(BF16), l1_w_out.astype(BF16))
```

```python
import functools
import math

import numpy as np
import jax
import jax.numpy as jnp
from jax import lax
from jax.experimental import pallas as pl
from jax.experimental.pallas import tpu as pltpu

F32 = jnp.float32
BF16 = jnp.bfloat16

D_MODEL = 2048
GRID_W = 64
HEAD_DIM = 128
N_Q_HEADS = 12
N_KV_HEADS = 4
GQA_GROUP = N_Q_HEADS // N_KV_HEADS
WINDOW = 128
ATTN_WIDTH = N_Q_HEADS * HEAD_DIM
KV_WIDTH = N_KV_HEADS * HEAD_DIM
FOURIER_WIDTH = D_MODEL - ATTN_WIDTH
FOURIER_GROUPS = 4
FOURIER_GROUP_DIM = FOURIER_WIDTH // FOURIER_GROUPS
IN0_WIDTH = ATTN_WIDTH + 2 * KV_WIDTH + FOURIER_WIDTH + D_MODEL
POOL_WINDOWS = (2, 4, 8, 16)
POOL_GROUPS = len(POOL_WINDOWS)
POOL_GROUP_DIM = D_MODEL // POOL_GROUPS
ROPE_BASE = 10000.0
ROPE_PAIRS = HEAD_DIM // 4
RMS_EPS = 1e-6

LANES = 128
SUBLANES = 8
BF16_SUBLANES = 16
VMEM_LIMIT_BYTES = 56 * 1024 * 1024

FFT_RADIX = 128
FFT_PITCH = FFT_RADIX + SUBLANES

COND_ROWS = 8


def _silu(x):
    return x * jax.nn.sigmoid(x)


def _compiler_params(semantics):
    return pltpu.CompilerParams(dimension_semantics=semantics, vmem_limit_bytes=VMEM_LIMIT_BYTES)


def _resident(block_shape, index_map):
    return pl.BlockSpec(block_shape, index_map, pipeline_mode=pl.Buffered(1))


@functools.lru_cache(maxsize=None)
def _rope_tables(seq):
    pos = np.arange(seq)
    row = (pos // GRID_W).astype(np.float64)
    col = (pos % GRID_W).astype(np.float64)
    inv = 1.0 / (ROPE_BASE ** (np.arange(ROPE_PAIRS, dtype=np.float64) / ROPE_PAIRS))
    ar, ac = row[:, None] * inv, col[:, None] * inv
    cos = np.concatenate([np.cos(ar), np.cos(ar), np.cos(ac), np.cos(ac)], axis=1)
    sin = np.concatenate([-np.sin(ar), np.sin(ar), -np.sin(ac), np.sin(ac)], axis=1)
    return cos.astype(np.float32), sin.astype(np.float32)


@functools.lru_cache(maxsize=None)
def _dft_tables():
    r = FFT_RADIX
    idx = np.arange(r)
    ang = 2.0 * np.pi * np.outer(idx, idx) / r
    c, s = np.cos(ang) / math.sqrt(r), np.sin(ang) / math.sqrt(r)
    stage1 = np.block([[c, s], [-s, c]])
    stage2 = np.concatenate([c, s], axis=1)
    chan = np.concatenate([c, -s], axis=0)
    ang2 = 2.0 * np.pi * ((idx[:, None] * idx[None, :]) % (r * r)) / (r * r)
    tw = np.stack([np.cos(ang2), np.sin(ang2)])[..., None] * np.ones(LANES)
    return (stage1.astype(np.float32), stage2.astype(np.float32), chan.astype(np.float32),
            tw.astype(np.float32))


@functools.lru_cache(maxsize=None)
def _pool_edge_tables(seq):
    first = np.zeros((SUBLANES, D_MODEL), np.float64)
    last = np.zeros((SUBLANES, D_MODEL), np.float64)
    for g, w in enumerate(POOL_WINDOWS):
        for r in range(SUBLANES):
            for tab, t in ((first, r), (last, seq - SUBLANES + r)):
                lo = min(max(t - w // 2, 0), seq)
                hi = min(max(t - w // 2 + w, 0), seq)
                tab[r, g * POOL_GROUP_DIM:(g + 1) * POOL_GROUP_DIM] = 1.0 / (hi - lo)
    interior = np.repeat(1.0 / np.array(POOL_WINDOWS, np.float64), POOL_GROUP_DIM)[None, :]
    return first.astype(np.float32), last.astype(np.float32), interior.astype(np.float32)


def _mod_kernel(cond_ref, w_ref, b_ref, o_ref):
    a = _silu(cond_ref[...]).astype(BF16)
    o_ref[...] = jnp.dot(a, w_ref[...].astype(BF16), preferred_element_type=F32) + b_ref[...]


def _modulation(cond, w_mod, b_mod):
    d, n = w_mod.shape
    tn = 1536
    return pl.pallas_call(
        _mod_kernel,
        grid=(n // tn,),
        in_specs=[
            pl.BlockSpec((COND_ROWS, d), lambda j: (0, 0)),
            pl.BlockSpec((d, tn), lambda j: (0, j)),
            pl.BlockSpec((1, tn), lambda j: (0, j)),
        ],
        out_specs=pl.BlockSpec((COND_ROWS, tn), lambda j: (0, j)),
        out_shape=jax.ShapeDtypeStruct((COND_ROWS, n), F32),
        compiler_params=_compiler_params(("arbitrary",)),
        name="modulation",
    )(cond, w_mod, b_mod.reshape(1, n))


def _norm_mod(x, g, scale, shift):
    ms = jnp.mean(x * x, axis=-1, keepdims=True)
    return x * lax.rsqrt(ms + RMS_EPS) * g * (1.0 + scale) + shift


IN0_CHUNK = 512


def _rope(x, cos, sin, partner_is_right):
    rot = jnp.where(partner_is_right, pltpu.roll(x, HEAD_DIM - ROPE_PAIRS, 1), pltpu.roll(x, ROPE_PAIRS, 1))
    return x * cos + rot * sin


def _inproj0_kernel(x_ref, g_ref, scale_ref, shift_ref, w_ref, cos_ref, sin_ref,
                    q_ref, k_ref, v_ref, f_ref, sz_ref, h_scr):
    h_scr[...] = _norm_mod(x_ref[0], g_ref[...], scale_ref[0], shift_ref[0]).astype(BF16)
    cos = cos_ref[...]
    sin = sin_ref[...]
    q_scale = HEAD_DIM ** -0.5 * math.log2(math.e)
    cos_q, sin_q = cos * q_scale, sin * q_scale
    lane = lax.broadcasted_iota(jnp.int32, cos.shape, 1)
    partner_is_right = (lane % (2 * ROPE_PAIRS)) < ROPE_PAIRS
    heads_per_chunk = IN0_CHUNK // HEAD_DIM

    for c0 in range(0, IN0_WIDTH, IN0_CHUNK):
        acc = jnp.dot(h_scr[...], w_ref[:, c0:c0 + IN0_CHUNK], preferred_element_type=F32)
        if c0 < ATTN_WIDTH + KV_WIDTH:
            is_q = c0 < ATTN_WIDTH
            dst, off = (q_ref, c0) if is_q else (k_ref, c0 - ATTN_WIDTH)
            cs, sn = (cos_q, sin_q) if is_q else (cos, sin)
            for j in range(heads_per_chunk):
                xh = acc[:, j * HEAD_DIM:(j + 1) * HEAD_DIM]
                lo = off + j * HEAD_DIM
                dst[0, :, lo:lo + HEAD_DIM] = _rope(xh, cs, sn, partner_is_right).astype(dst.dtype)
        elif c0 < ATTN_WIDTH + 2 * KV_WIDTH:
            v_ref[0] = acc.astype(v_ref.dtype)
        elif c0 < ATTN_WIDTH + 2 * KV_WIDTH + FOURIER_WIDTH:
            f_ref[0] = acc
        else:
            off = c0 - (IN0_WIDTH - D_MODEL)
            sz_ref[0, :, off:off + IN0_CHUNK] = _silu(acc).astype(sz_ref.dtype)


def _inproj0(x, norm_g, scale, shift, w_in, cos, sin, tm=512):
    b, s, d = x.shape
    vec = pl.BlockSpec((1, 1, d), lambda bi, i: (bi, 0, 0))
    tok = lambda width: pl.BlockSpec((1, tm, width), lambda bi, i: (bi, i, 0))
    tab = pl.BlockSpec((tm, HEAD_DIM), lambda bi, i: (i, 0))
    return pl.pallas_call(
        _inproj0_kernel,
        grid=(b, s // tm),
        in_specs=[tok(d), _resident((1, d), lambda bi, i: (0, 0)), vec, vec,
                  _resident((d, IN0_WIDTH), lambda bi, i: (0, 0)), tab, tab],
        out_specs=[tok(ATTN_WIDTH), tok(KV_WIDTH), tok(KV_WIDTH), tok(FOURIER_WIDTH), tok(D_MODEL)],
        out_shape=[
            jax.ShapeDtypeStruct((b, s, ATTN_WIDTH), BF16),
            jax.ShapeDtypeStruct((b, s, KV_WIDTH), BF16),
            jax.ShapeDtypeStruct((b, s, KV_WIDTH), BF16),
            jax.ShapeDtypeStruct((b, s, FOURIER_WIDTH), F32),
            jax.ShapeDtypeStruct((b, s, D_MODEL), BF16),
        ],
        scratch_shapes=[pltpu.VMEM((tm, d), BF16)],
        compiler_params=_compiler_params(("parallel", "parallel")),
        name="inproj0",
    )(x, norm_g.reshape(1, d), scale, shift, w_in, cos, sin)


def _ctx_kv_kernel(c_ref, g_ref, scale_ref, shift_ref, wk_ref, wv_ref, k_ref, v_ref):
    h = _norm_mod(c_ref[0], g_ref[...], scale_ref[...], shift_ref[...]).astype(BF16)
    k_ref[0] = jnp.dot(h, wk_ref[...], preferred_element_type=F32).astype(k_ref.dtype)
    v_ref[0] = jnp.dot(h, wv_ref[...], preferred_element_type=F32).astype(v_ref.dtype)


def _ctx_kv(ctx, norm_g, scale_c, shift_c, w_in):
    b, c, d = ctx.shape
    vec = pl.BlockSpec((1, d), lambda bi: (0, 0))
    kblk = ATTN_WIDTH // KV_WIDTH
    out = pl.BlockSpec((1, c, KV_WIDTH), lambda bi: (bi, 0, 0))
    return pl.pallas_call(
        _ctx_kv_kernel,
        grid=(b,),
        in_specs=[pl.BlockSpec((1, c, d), lambda bi: (bi, 0, 0)), vec, vec, vec,
                  pl.BlockSpec((d, KV_WIDTH), lambda bi: (0, kblk)),
                  pl.BlockSpec((d, KV_WIDTH), lambda bi: (0, kblk + 1))],
        out_specs=[out, out],
        out_shape=[jax.ShapeDtypeStruct((b, c, KV_WIDTH), BF16)] * 2,
        compiler_params=_compiler_params(("arbitrary",)),
        name="ctx_kv",
    )(ctx, norm_g.reshape(1, d), scale_c, shift_c, w_in, w_in)


ATTN_TQ = 1024
Q_ROWS = GQA_GROUP * WINDOW


def _head_cols(h):
    return slice(h * HEAD_DIM, (h + 1) * HEAD_DIM)


def _attn_kernel(sink_ref, q_ref, k_ref, kp_ref, kn_ref, v_ref, vp_ref, vn_ref, kc_ref, vc_ref,
                 o_ref, kext, vaug, vcaug, s0, s1, p0, p1, d0, d1):
    i = pl.program_id(1)
    nb = ATTN_TQ // WINDOW
    nblk_total = pl.num_programs(1) * nb
    n_ctx = kc_ref.shape[1]
    s_scr, p_scr, d_scr = (s0, s1), (p0, p1), (d0, d1)

    kext[0:WINDOW] = kp_ref[0]
    kext[WINDOW:WINDOW + ATTN_TQ] = k_ref[0]
    kext[WINDOW + ATTN_TQ:] = kn_ref[0]
    ones = jnp.ones((ATTN_TQ + 2 * WINDOW, HEAD_DIM), BF16)
    for kh in range(N_KV_HEADS):
        vaug[kh, 0:WINDOW, 0:HEAD_DIM] = vp_ref[0, :, _head_cols(kh)]
        vaug[kh, WINDOW:WINDOW + ATTN_TQ, 0:HEAD_DIM] = v_ref[0, :, _head_cols(kh)]
        vaug[kh, WINDOW + ATTN_TQ:, 0:HEAD_DIM] = vn_ref[0, :, _head_cols(kh)]
        vaug[kh, :, HEAD_DIM:] = ones
        vcaug[kh, :, 0:HEAD_DIM] = vc_ref[0, :, _head_cols(kh)]
        vcaug[kh, :, HEAD_DIM:] = ones[0:n_ctx]

    r = lax.broadcasted_iota(jnp.int32, (Q_ROWS, WINDOW), 0) % WINDOW
    c = lax.broadcasted_iota(jnp.int32, (Q_ROWS, WINDOW), 1)
    neg = jnp.float32(-jnp.inf)
    bias_prev = jnp.where(c >= r, 0.0, neg).astype(F32)
    bias_next = jnp.where(c <= r, 0.0, neg).astype(F32)
    nt = (((1,), (1,)), ((), ()))

    def scores(j, kh, s_ref):
        base = pl.multiple_of(j * WINDOW, WINDOW)
        n = i * nb + j
        q3 = jnp.concatenate([q_ref[0, pl.ds(base, WINDOW), _head_cols(kh * GQA_GROUP + g)]
                              for g in range(GQA_GROUP)], axis=0)
        s_ctx = lax.dot_general(q3, kc_ref[0, :, _head_cols(kh)], nt, preferred_element_type=F32)
        s_win = lax.dot_general(q3, kext[pl.ds(base, 3 * WINDOW), _head_cols(kh)], nt,
                                preferred_element_type=F32)
        s_ref[kh, :, 0:n_ctx] = s_ctx
        s_ref[kh, :, n_ctx:n_ctx + WINDOW] = s_win[:, 0:WINDOW] + jnp.where(n > 0, bias_prev, neg)
        s_ref[kh, :, n_ctx + WINDOW:n_ctx + 2 * WINDOW] = s_win[:, WINDOW:2 * WINDOW]
        s_ref[kh, :, n_ctx + 2 * WINDOW:] = s_win[:, 2 * WINDOW:] + jnp.where(n < nblk_total - 1, bias_next, neg)

    def softmax(kh, s_ref, p_ref, d_ref):
        ntile = s_ref.shape[2] // LANES
        tiles = [s_ref[kh, :, t * LANES:(t + 1) * LANES] for t in range(ntile)]
        mx = tiles[0]
        for t in range(1, ntile):
            mx = jnp.maximum(mx, tiles[t])
        sink = jnp.concatenate([jnp.full((WINDOW, 1), sink_ref[kh * GQA_GROUP + g] * math.log2(math.e), F32)
                                for g in range(GQA_GROUP)], axis=0)
        m = jnp.maximum(jnp.max(mx, axis=1, keepdims=True), sink)
        for t in range(ntile):
            p_ref[kh, :, t * LANES:(t + 1) * LANES] = jnp.exp2(tiles[t] - m).astype(BF16)
        d_ref[kh] = jnp.broadcast_to(jnp.exp2(sink - m), (Q_ROWS, HEAD_DIM))

    def values(j, kh, p_ref, d_ref):
        base = pl.multiple_of(j * WINDOW, WINDOW)
        o = jnp.dot(p_ref[kh, :, 0:n_ctx], vcaug[kh], preferred_element_type=F32)
        o = o + jnp.dot(p_ref[kh, :, n_ctx:], vaug[kh, pl.ds(base, 3 * WINDOW), :], preferred_element_type=F32)
        res = o[:, 0:HEAD_DIM] * (1.0 / (o[:, HEAD_DIM:] + d_ref[kh]))
        for g in range(GQA_GROUP):
            o_ref[0, pl.ds(base, WINDOW), _head_cols(kh * GQA_GROUP + g)] = (
                res[g * WINDOW:(g + 1) * WINDOW].astype(o_ref.dtype))

    def tick(t, do_scores=True, do_softmax=True, do_values=True, parity=None):
        par = t % 2 if parity is None else parity
        for kh in range(N_KV_HEADS):
            if do_scores:
                scores(t, kh, s_scr[par])
            if do_softmax:
                softmax(kh, s_scr[1 - par], p_scr[1 - par], d_scr[1 - par])
            if do_values:
                values(t - 2, kh, p_scr[par], d_scr[par])

    tick(0, do_softmax=False, do_values=False)
    tick(1, do_values=False)

    def pair(jj, carry):
        tick(2 + 2 * jj, parity=0)
        tick(3 + 2 * jj, parity=1)
        return carry

    lax.fori_loop(0, (nb - 2) // 2, pair, 0)
    tick(nb, do_scores=False)
    tick(nb + 1, do_scores=False, do_softmax=False)


def _attention(q, k, v, kc, vc, sink):
    b, s, _ = q.shape
    c = kc.shape[1]
    steps = s // ATTN_TQ
    per = ATTN_TQ // WINDOW
    nblk = s // WINDOW
    main = lambda width: pl.BlockSpec((1, ATTN_TQ, width), lambda bi, i: (bi, i, 0))
    prev = pl.BlockSpec((1, WINDOW, KV_WIDTH), lambda bi, i: (bi, jnp.maximum(i * per - 1, 0), 0))
    nxt = pl.BlockSpec((1, WINDOW, KV_WIDTH), lambda bi, i: (bi, jnp.minimum((i + 1) * per, nblk - 1), 0))
    ctx = pl.BlockSpec((1, c, KV_WIDTH), lambda bi, i: (bi, 0, 0))
    return pl.pallas_call(
        _attn_kernel,
        grid=(b, steps),
        in_specs=[pl.BlockSpec(memory_space=pltpu.SMEM), main(ATTN_WIDTH),
                  main(KV_WIDTH), prev, nxt, main(KV_WIDTH), prev, nxt, ctx, ctx],
        out_specs=main(ATTN_WIDTH),
        out_shape=jax.ShapeDtypeStruct((b, s, ATTN_WIDTH), BF16),
        scratch_shapes=[pltpu.VMEM((ATTN_TQ + 2 * WINDOW, KV_WIDTH), BF16),
                        pltpu.VMEM((N_KV_HEADS, ATTN_TQ + 2 * WINDOW, 2 * HEAD_DIM), BF16),
                        pltpu.VMEM((N_KV_HEADS, c, 2 * HEAD_DIM), BF16)]
                       + [pltpu.VMEM((N_KV_HEADS, Q_ROWS, c + 3 * WINDOW), F32)] * 2
                       + [pltpu.VMEM((N_KV_HEADS, Q_ROWS, c + 3 * WINDOW), BF16)] * 2
                       + [pltpu.VMEM((N_KV_HEADS, Q_ROWS, HEAD_DIM), F32)] * 2,
        compiler_params=_compiler_params(("parallel", "parallel")),
        name="window_attn",
    )(sink, q, k, k, k, v, v, v, kc, vc)


FFT_HALF = SUBLANES
FFT_STEP = 2 * FFT_HALF
FFT_STEPS = FFT_RADIX // FFT_STEP


def _fourier_kernel(xa_ref, xb_ref, s1_ref, s2_ref, chan_ref, wf_ref, tw_ref, o_ref,
                    hr_scr, hi_scr, w_scr, yra, yia, yrb, yib, za, zb):
    r = FFT_RADIX
    t = pl.program_id(2)

    @pl.when(t == 0)
    def _():
        w = jnp.dot(chan_ref[...], wf_ref[0].astype(BF16), preferred_element_type=F32)
        w_scr[:, 0:r] = w[0:r].astype(BF16)
        w_scr[:, r:2 * r] = w[r:2 * r].astype(BF16)

    halves = ((xa_ref, yra, yia, za), (xb_ref, yrb, yib, zb))

    @pl.when(t < FFT_STEPS)
    def _():
        for x_ref, yr, yi, _ in halves:
            x2 = x_ref[...].reshape(r * FFT_HALF, LANES).astype(BF16)
            y = jnp.dot(x2, w_scr[...], preferred_element_type=F32)
            yr[...] = y[:, 0:r]
            yi[...] = y[:, r:2 * r]
        hs = []
        for _, yr, yi, _ in halves:
            top = jnp.concatenate([yr[pl.ds(j, r, stride=FFT_HALF), :].astype(BF16) for j in range(FFT_HALF)], axis=1)
            bot = jnp.concatenate([yi[pl.ds(j, r, stride=FFT_HALF), :].astype(BF16) for j in range(FFT_HALF)], axis=1)
            hs.append(jnp.dot(s1_ref[...], jnp.concatenate([top, bot], axis=0), preferred_element_type=F32))
        for half, h in enumerate(hs):
            for j in range(FFT_HALF):
                n1 = (2 * t + half) * FFT_HALF + j
                row = pl.multiple_of(n1 * FFT_PITCH, SUBLANES)
                hr_scr[pl.ds(row, r), :] = h[0:r, j * LANES:(j + 1) * LANES]
                hi_scr[pl.ds(row, r), :] = h[r:2 * r, j * LANES:(j + 1) * LANES]

    @pl.when(t >= FFT_STEPS)
    def _():
        zs = []
        for half in range(2):
            tops, bots = [], []
            for j in range(half * FFT_HALF, (half + 1) * FFT_HALF):
                v = (t - FFT_STEPS) * FFT_STEP + j
                hr = hr_scr[pl.ds(v, r, stride=FFT_PITCH), :]
                hi = hi_scr[pl.ds(v, r, stride=FFT_PITCH), :]
                tc = tw_ref[0, v].astype(F32)
                ts = tw_ref[1, v].astype(F32)
                tops.append((hr * tc + hi * ts).astype(BF16))
                bots.append((hi * tc - hr * ts).astype(BF16))
            rhs = jnp.concatenate([jnp.concatenate(tops, axis=1), jnp.concatenate(bots, axis=1)], axis=0)
            zs.append(jnp.dot(s2_ref[...], rhs, preferred_element_type=F32))
        for half, z in enumerate(zs):
            z_scr = halves[half][3]
            for j in range(FFT_HALF):
                z_scr[pl.ds(j, r, stride=FFT_HALF), :] = z[:, j * LANES:(j + 1) * LANES]
            o_ref[:, half * FFT_HALF:(half + 1) * FFT_HALF, :] = z_scr[...].reshape(r, FFT_HALF, LANES)


def _fourier_mix(f, w_f):
    b, s, _ = f.shape
    r = FFT_RADIX
    assert s == r * r
    s1, s2, chan, tw = _dft_tables()
    f5 = f.reshape(b, r, r // FFT_HALF, FFT_HALF, FOURIER_WIDTH)
    last = r // FFT_HALF - 1
    x_spec = lambda half: pl.BlockSpec(
        (None, r, None, FFT_HALF, LANES), lambda bi, g, t: (bi, 0, jnp.minimum(2 * t + half, last), 0, g))
    o_spec = pl.BlockSpec((None, r, None, FFT_STEP, LANES),
                          lambda bi, g, t: (bi, 0, jnp.maximum(t - FFT_STEPS, 0), 0, g))
    const = lambda shape: _resident(shape, lambda bi, g, t: (0,) * len(shape))
    z5 = pl.pallas_call(
        _fourier_kernel,
        grid=(b, FOURIER_GROUPS, 2 * FFT_STEPS),
        in_specs=[x_spec(0), x_spec(1), const((2 * r, 2 * r)), const((r, 2 * r)), const((2 * r, r)),
                  pl.BlockSpec((1, r, r), lambda bi, g, t: (g, 0, 0)), const((2, r, r, LANES))],
        out_specs=o_spec,
        out_shape=jax.ShapeDtypeStruct((b, r, FFT_STEPS, FFT_STEP, FOURIER_WIDTH), F32),
        scratch_shapes=[pltpu.VMEM((r * FFT_PITCH, LANES), F32), pltpu.VMEM((r * FFT_PITCH, LANES), F32),
                        pltpu.VMEM((r, 2 * r), BF16)]
                       + [pltpu.VMEM((r * FFT_HALF, LANES), F32)] * 6,
        compiler_params=_compiler_params(("parallel", "parallel", "arbitrary")),
        name="fourier_mix",
    )(f5, f5, *(jnp.asarray(a).astype(BF16) for a in (s1, s2, chan)), w_f, jnp.asarray(tw).astype(BF16))
    return z5.reshape(b, s, FOURIER_WIDTH)


def _outproj0_kernel(a_ref, zf_ref, sz_ref, x_ref, gate_ref, w_ref, o_ref):
    ya = a_ref[0] * sz_ref[0, :, 0:ATTN_WIDTH]
    yf = zf_ref[0].astype(BF16) * sz_ref[0, :, ATTN_WIDTH:]
    acc = jnp.dot(ya, w_ref[0:ATTN_WIDTH, :], preferred_element_type=F32)
    acc = acc + jnp.dot(yf, w_ref[ATTN_WIDTH:, :], preferred_element_type=F32)
    o_ref[0] = x_ref[0] + gate_ref[0] * acc


def _outproj0(attn, zf, sz, x, gate, w_out, tm=512):
    b, s, d = x.shape
    tok = lambda width: pl.BlockSpec((1, tm, width), lambda bi, i: (bi, i, 0))
    return pl.pallas_call(
        _outproj0_kernel,
        grid=(b, s // tm),
        in_specs=[tok(ATTN_WIDTH), tok(FOURIER_WIDTH), tok(d), tok(d),
                  pl.BlockSpec((1, 1, d), lambda bi, i: (bi, 0, 0)),
                  _resident((d, d), lambda bi, i: (0, 0))],
        out_specs=tok(d),
        out_shape=jax.ShapeDtypeStruct((b, s, d), F32),
        compiler_params=_compiler_params(("parallel", "parallel")),
        name="outproj0",
    )(attn, zf, sz, x, gate, w_out)


IN1_CHUNK = 512


def _inproj1_kernel(x_ref, g_ref, scale_ref, shift_ref, w_ref, u_ref, sz_ref, h_scr):
    h_scr[...] = _norm_mod(x_ref[0], g_ref[...], scale_ref[0], shift_ref[0]).astype(BF16)
    for c0 in range(0, 2 * D_MODEL, IN1_CHUNK):
        acc = jnp.dot(h_scr[...], w_ref[:, c0:c0 + IN1_CHUNK], preferred_element_type=F32)
        if c0 < D_MODEL:
            u_ref[0, :, c0:c0 + IN1_CHUNK] = acc.astype(u_ref.dtype)
        else:
            sz_ref[0, :, c0 - D_MODEL:c0 - D_MODEL + IN1_CHUNK] = _silu(acc).astype(sz_ref.dtype)


def _inproj1(x, norm_g, scale, shift, w_in, tm=512):
    b, s, d = x.shape
    vec = pl.BlockSpec((1, 1, d), lambda bi, i: (bi, 0, 0))
    tok = pl.BlockSpec((1, tm, d), lambda bi, i: (bi, i, 0))
    return pl.pallas_call(
        _inproj1_kernel,
        grid=(b, s // tm),
        in_specs=[tok, _resident((1, d), lambda bi, i: (0, 0)), vec, vec,
                  _resident((d, 2 * d), lambda bi, i: (0, 0))],
        out_specs=[tok, tok],
        out_shape=[jax.ShapeDtypeStruct((b, s, d), BF16)] * 2,
        scratch_shapes=[pltpu.VMEM((tm, d), BF16)],
        compiler_params=_compiler_params(("parallel", "parallel")),
        name="inproj1",
    )(x, norm_g.reshape(1, d), scale, shift, w_in)


POOL_HALO = BF16_SUBLANES
POOL_SUB = 128
POOL_PAD = 64


@functools.lru_cache(maxsize=None)
def _pool_band():
    p = np.arange(POOL_SUB)[:, None]
    e = np.arange(POOL_SUB + 2 * POOL_PAD)[None, :] - POOL_PAD
    return np.stack([((e >= p - w // 2) & (e <= p - w // 2 + w - 1)) for w in POOL_WINDOWS]).astype(np.float32)


def _pool_out_kernel(u_ref, up_ref, un_ref, sz_ref, x_ref, gate_ref, ls_ref, fn_ref,
                     inv_first_ref, inv_last_ref, inv_mid_ref, band_ref, wg_ref, wo_ref, o_ref, ext_scr, y_scr):
    i = pl.program_id(1)
    last = pl.num_programs(1) - 1
    tm = u_ref.shape[1]
    pad = POOL_PAD
    zeros = jnp.zeros((pad - POOL_HALO, u_ref.shape[2]), BF16)
    ext_scr[0:pad - POOL_HALO] = zeros
    ext_scr[pad - POOL_HALO:pad] = jnp.where(i > 0, up_ref[0], jnp.zeros_like(up_ref[0]))
    ext_scr[pad:pad + tm] = u_ref[0]
    ext_scr[pad + tm:pad + tm + POOL_HALO] = jnp.where(i < last, un_ref[0], jnp.zeros_like(un_ref[0]))
    ext_scr[pad + tm + POOL_HALO:] = zeros
    for g, w in enumerate(POOL_WINDOWS):
        cols = slice(g * POOL_GROUP_DIM, (g + 1) * POOL_GROUP_DIM)
        uc = u_ref[0, :, cols].astype(F32)
        wsum = jnp.concatenate(
            [jnp.dot(band_ref[g], ext_scr[r0:r0 + POOL_SUB + 2 * pad, cols], preferred_element_type=F32)
             for r0 in range(0, tm, POOL_SUB)], axis=0)
        inv_mid = inv_mid_ref[:, cols]
        inv_head = jnp.where(i == 0, inv_first_ref[:, cols], inv_mid)
        inv_tail = jnp.where(i == last, inv_last_ref[:, cols], inv_mid)
        mean = jnp.concatenate([wsum[0:SUBLANES] * inv_head,
                                wsum[SUBLANES:tm - SUBLANES] * inv_mid,
                                wsum[tm - SUBLANES:] * inv_tail], axis=0)
        pooled = (mean - uc).astype(BF16)
        y_scr[:, cols] = jnp.dot(pooled, wg_ref[g], preferred_element_type=F32)

    yz = (y_scr[...] * ls_ref[...]).astype(BF16) * sz_ref[0]
    acc = jnp.dot(yz, wo_ref[...], preferred_element_type=F32)
    x2 = x_ref[0] + gate_ref[0] * acc
    ms = jnp.mean(x2 * x2, axis=-1, keepdims=True)
    o_ref[0] = x2 * lax.rsqrt(ms + RMS_EPS) * fn_ref[...]


def _pool_out(u, sz, x, gate, layer_scale, final_norm, w_grp, w_out, tm=512):
    b, s, d = x.shape
    per = tm // POOL_HALO
    nh = s // POOL_HALO
    inv_first, inv_last, inv_mid = _pool_edge_tables(s)
    tok = pl.BlockSpec((1, tm, d), lambda bi, i: (bi, i, 0))
    prev = pl.BlockSpec((1, POOL_HALO, d), lambda bi, i: (bi, jnp.maximum(i * per - 1, 0), 0))
    nxt = pl.BlockSpec((1, POOL_HALO, d), lambda bi, i: (bi, jnp.minimum((i + 1) * per, nh - 1), 0))
    row = lambda rows: _resident((rows, d), lambda bi, i: (0, 0))
    return pl.pallas_call(
        _pool_out_kernel,
        grid=(b, s // tm),
        in_specs=[tok, prev, nxt, tok, tok, pl.BlockSpec((1, 1, d), lambda bi, i: (bi, 0, 0)),
                  row(1), row(1), row(SUBLANES), row(SUBLANES), row(1),
                  _resident((POOL_GROUPS, POOL_SUB, POOL_SUB + 2 * POOL_PAD), lambda bi, i: (0, 0, 0)),
                  _resident((POOL_GROUPS, POOL_GROUP_DIM, POOL_GROUP_DIM), lambda bi, i: (0, 0, 0)),
                  _resident((d, d), lambda bi, i: (0, 0))],
        out_specs=tok,
        out_shape=jax.ShapeDtypeStruct((b, s, d), F32),
        scratch_shapes=[pltpu.VMEM((tm + 2 * POOL_PAD, d), BF16), pltpu.VMEM((tm, d), F32)],
        compiler_params=_compiler_params(("parallel", "parallel")),
        name="pool_out",
    )(u, u, u, sz, x, gate, layer_scale.reshape(1, d), final_norm.reshape(1, d),
      jnp.asarray(inv_first), jnp.asarray(inv_last), jnp.asarray(inv_mid), jnp.asarray(_pool_band()).astype(BF16),
      w_grp, w_out)


def kernel(x, c, ctx, c_ctx, l0_norm, l0_w_mod, l0_b_mod, l0_w_in, l0_sink, l0_w_f, l0_w_out,
           l1_norm, l1_w_mod, l1_b_mod, l1_w_in, l1_w_grp, l1_scale, l1_w_out, final_norm):
    b, s, d = x.shape
    cond = jnp.concatenate([c, c_ctx[None, :], jnp.zeros((COND_ROWS - b - 1, d), F32)], axis=0)
    m0 = _modulation(cond, l0_w_mod, l0_b_mod)
    m1 = _modulation(cond, l1_w_mod, l1_b_mod)
    tok_vec = lambda m, k: m[:b, k * d:(k + 1) * d].reshape(b, 1, d)
    shift0, scale0, gate0 = (tok_vec(m0, k) for k in range(3))
    shift1, scale1, gate1 = (tok_vec(m1, k) for k in range(3))
    shift_c, scale_c = m0[b:b + 1, 0:d], m0[b:b + 1, d:2 * d]

    w_in0 = l0_w_in.astype(BF16)
    cos, sin = _rope_tables(s)
    q, k, v, f, sz0 = _inproj0(x, l0_norm, scale0, shift0, w_in0, jnp.asarray(cos), jnp.asarray(sin))
    kc, vc = _ctx_kv(ctx, l0_norm, scale_c, shift_c, w_in0)
    attn = _attention(q, k, v, kc, vc, l0_sink)
    zf = _fourier_mix(f, l0_w_f)
    x1 = _outproj0(attn, zf, sz0, x, gate0, l0_w_out.astype(BF16))

    u, sz1 = _inproj1(x1, l1_norm, scale1, shift1, l1_w_in.astype(BF16))
    return _pool_out(u, sz1, x1, gate1, l1_scale, final_norm, l1_w_grp.astype(BF16), l1_w_out.astype(BF16))
```

```python
import functools
import math

import numpy as np
import jax
import jax.numpy as jnp
from jax import lax
from jax.experimental import pallas as pl
from jax.experimental.pallas import tpu as pltpu

F32 = jnp.float32
BF16 = jnp.bfloat16

D_MODEL = 2048
GRID_W = 64
HEAD_DIM = 128
N_Q_HEADS = 12
N_KV_HEADS = 4
GQA_GROUP = N_Q_HEADS // N_KV_HEADS
WINDOW = 128
ATTN_WIDTH = N_Q_HEADS * HEAD_DIM
KV_WIDTH = N_KV_HEADS * HEAD_DIM
FOURIER_WIDTH = D_MODEL - ATTN_WIDTH
FOURIER_GROUPS = 4
FOURIER_GROUP_DIM = FOURIER_WIDTH // FOURIER_GROUPS
IN0_WIDTH = ATTN_WIDTH + 2 * KV_WIDTH + FOURIER_WIDTH + D_MODEL
POOL_WINDOWS = (2, 4, 8, 16)
POOL_GROUPS = len(POOL_WINDOWS)
POOL_GROUP_DIM = D_MODEL // POOL_GROUPS
ROPE_BASE = 10000.0
ROPE_PAIRS = HEAD_DIM // 4
RMS_EPS = 1e-6

LANES = 128
SUBLANES = 8
BF16_SUBLANES = 16
VMEM_LIMIT_BYTES = 56 * 1024 * 1024

FFT_RADIX = 128
FFT_PITCH = FFT_RADIX + SUBLANES

COND_ROWS = 8


ROW_SPLIT = 2


def _silu(x):
    hx = 0.5 * x
    return hx + hx * jnp.tanh(hx)


def _compiler_params(semantics):
    return pltpu.CompilerParams(dimension_semantics=semantics, vmem_limit_bytes=VMEM_LIMIT_BYTES)


def _resident(block_shape, index_map):
    return pl.BlockSpec(block_shape, index_map, pipeline_mode=pl.Buffered(1))


@functools.lru_cache(maxsize=None)
def _rope_tables(seq):
    pos = np.arange(seq)
    row = (pos // GRID_W).astype(np.float64)
    col = (pos % GRID_W).astype(np.float64)
    inv = 1.0 / (ROPE_BASE ** (np.arange(ROPE_PAIRS, dtype=np.float64) / ROPE_PAIRS))
    ar, ac = row[:, None] * inv, col[:, None] * inv
    cos = np.concatenate([np.cos(ar), np.cos(ar), np.cos(ac), np.cos(ac)], axis=1)
    sin = np.concatenate([-np.sin(ar), np.sin(ar), -np.sin(ac), np.sin(ac)], axis=1)
    return cos.astype(np.float32), sin.astype(np.float32)


@functools.lru_cache(maxsize=None)
def _dft_tables():
    r = FFT_RADIX
    idx = np.arange(r)
    ang = 2.0 * np.pi * np.outer(idx, idx) / r
    c, s = np.cos(ang) / math.sqrt(r), np.sin(ang) / math.sqrt(r)
    stage1 = np.block([[c, s], [-s, c]])
    stage2 = np.concatenate([c, s], axis=1)
    chan = np.concatenate([c, -s], axis=0)
    ang2 = 2.0 * np.pi * ((idx[:, None] * idx[None, :]) % (r * r)) / (r * r)
    tw = np.stack([np.cos(ang2), np.sin(ang2)])[..., None] * np.ones(LANES)
    return (stage1.astype(np.float32), stage2.astype(np.float32), chan.astype(np.float32),
            tw.astype(np.float32))


@functools.lru_cache(maxsize=None)
def _pool_edge_tables(seq):
    first = np.zeros((SUBLANES, D_MODEL), np.float64)
    last = np.zeros((SUBLANES, D_MODEL), np.float64)
    for g, w in enumerate(POOL_WINDOWS):
        for r in range(SUBLANES):
            for tab, t in ((first, r), (last, seq - SUBLANES + r)):
                lo = min(max(t - w // 2, 0), seq)
                hi = min(max(t - w // 2 + w, 0), seq)
                tab[r, g * POOL_GROUP_DIM:(g + 1) * POOL_GROUP_DIM] = 1.0 / (hi - lo)
    interior = np.repeat(1.0 / np.array(POOL_WINDOWS, np.float64), POOL_GROUP_DIM)[None, :]
    return first.astype(np.float32), last.astype(np.float32), interior.astype(np.float32)


def _mod_kernel(cond_ref, w_ref, b_ref, o_ref):
    a = _silu(cond_ref[...]).astype(BF16)
    o_ref[...] = jnp.dot(a, w_ref[...].astype(BF16), preferred_element_type=F32) + b_ref[...]


def _modulation(cond, w_mod, b_mod):
    d, n = w_mod.shape
    tn = 1536
    return pl.pallas_call(
        _mod_kernel,
        grid=(n // tn,),
        in_specs=[
            pl.BlockSpec((COND_ROWS, d), lambda j: (0, 0)),
            pl.BlockSpec((d, tn), lambda j: (0, j)),
            pl.BlockSpec((1, tn), lambda j: (0, j)),
        ],
        out_specs=pl.BlockSpec((COND_ROWS, tn), lambda j: (0, j)),
        out_shape=jax.ShapeDtypeStruct((COND_ROWS, n), F32),
        compiler_params=_compiler_params(("arbitrary",)),
        name="modulation",
    )(cond, w_mod, b_mod.reshape(1, n))


def _norm_mod(x, g, scale, shift):
    ms = jnp.mean(x * x, axis=-1, keepdims=True)
    return x * lax.rsqrt(ms + RMS_EPS) * g * (1.0 + scale) + shift


IN0_CHUNK = 512


def _rope(x, cos, sin, partner_is_right):
    rot = jnp.where(partner_is_right, pltpu.roll(x, HEAD_DIM - ROPE_PAIRS, 1), pltpu.roll(x, ROPE_PAIRS, 1))
    return x * cos + rot * sin


def _inproj0_kernel(x_ref, g_ref, scale_ref, shift_ref, w_ref, cos_ref, sin_ref,
                    q_ref, k_ref, v_ref, f_ref, sz_ref, h_scr):
    tm = x_ref.shape[1]
    q_scale = HEAD_DIM ** -0.5 * math.log2(math.e)
    lane = lax.broadcasted_iota(jnp.int32, (tm // ROW_SPLIT, HEAD_DIM), 1)
    partner_is_right = (lane % (2 * ROPE_PAIRS)) < ROPE_PAIRS
    heads_per_chunk = IN0_CHUNK // HEAD_DIM
    row_sets = [slice(r0, r0 + tm // ROW_SPLIT) for r0 in range(0, tm, tm // ROW_SPLIT)]

    def normalize(rows):
        h_scr[rows, :] = _norm_mod(x_ref[0, rows, :], g_ref[...], scale_ref[0], shift_ref[0]).astype(BF16)

    normalize(row_sets[0])
    for c0 in range(0, IN0_WIDTH, IN0_CHUNK):
        for si, rows in enumerate(row_sets):
            acc = jnp.dot(h_scr[rows, :], w_ref[:, c0:c0 + IN0_CHUNK], preferred_element_type=F32)
            if c0 == 0 and si + 1 < len(row_sets):
                normalize(row_sets[si + 1])
            if c0 < ATTN_WIDTH + KV_WIDTH:
                is_q = c0 < ATTN_WIDTH
                dst, off = (q_ref, c0) if is_q else (k_ref, c0 - ATTN_WIDTH)
                cs, sn = cos_ref[rows, :], sin_ref[rows, :]
                if is_q:
                    cs, sn = cs * q_scale, sn * q_scale
                for j in range(heads_per_chunk):
                    xh = acc[:, j * HEAD_DIM:(j + 1) * HEAD_DIM]
                    lo = off + j * HEAD_DIM
                    dst[0, rows, lo:lo + HEAD_DIM] = _rope(xh, cs, sn, partner_is_right).astype(dst.dtype)
            elif c0 < ATTN_WIDTH + 2 * KV_WIDTH:
                v_ref[0, rows, :] = acc.astype(v_ref.dtype)
            elif c0 < ATTN_WIDTH + 2 * KV_WIDTH + FOURIER_WIDTH:
                f_ref[0, rows, :] = acc
            else:
                off = c0 - (IN0_WIDTH - D_MODEL)
                sz_ref[0, rows, off:off + IN0_CHUNK] = _silu(acc).astype(sz_ref.dtype)


def _inproj0(x, norm_g, scale, shift, w_in, cos, sin, tm=512):
    b, s, d = x.shape
    vec = pl.BlockSpec((1, 1, d), lambda bi, i: (bi, 0, 0))
    tok = lambda width: pl.BlockSpec((1, tm, width), lambda bi, i: (bi, i, 0))
    tab = pl.BlockSpec((tm, HEAD_DIM), lambda bi, i: (i, 0))
    return pl.pallas_call(
        _inproj0_kernel,
        grid=(b, s // tm),
        in_specs=[tok(d), _resident((1, d), lambda bi, i: (0, 0)), vec, vec,
                  _resident((d, IN0_WIDTH), lambda bi, i: (0, 0)), tab, tab],
        out_specs=[tok(ATTN_WIDTH), tok(KV_WIDTH), tok(KV_WIDTH), tok(FOURIER_WIDTH), tok(D_MODEL)],
        out_shape=[
            jax.ShapeDtypeStruct((b, s, ATTN_WIDTH), BF16),
            jax.ShapeDtypeStruct((b, s, KV_WIDTH), BF16),
            jax.ShapeDtypeStruct((b, s, KV_WIDTH), BF16),
            jax.ShapeDtypeStruct((b, s, FOURIER_WIDTH), F32),
            jax.ShapeDtypeStruct((b, s, D_MODEL), BF16),
        ],
        scratch_shapes=[pltpu.VMEM((tm, d), BF16)],
        compiler_params=_compiler_params(("parallel", "parallel")),
        name="inproj0",
    )(x, norm_g.reshape(1, d), scale, shift, w_in, cos, sin)


def _ctx_kv_kernel(c_ref, g_ref, scale_ref, shift_ref, wk_ref, wv_ref, k_ref, v_ref):
    h = _norm_mod(c_ref[0], g_ref[...], scale_ref[...], shift_ref[...]).astype(BF16)
    k_ref[0] = jnp.dot(h, wk_ref[...], preferred_element_type=F32).astype(k_ref.dtype)
    v_ref[0] = jnp.dot(h, wv_ref[...], preferred_element_type=F32).astype(v_ref.dtype)


def _ctx_kv(ctx, norm_g, scale_c, shift_c, w_in):
    b, c, d = ctx.shape
    vec = pl.BlockSpec((1, d), lambda bi: (0, 0))
    kblk = ATTN_WIDTH // KV_WIDTH
    out = pl.BlockSpec((1, c, KV_WIDTH), lambda bi: (bi, 0, 0))
    return pl.pallas_call(
        _ctx_kv_kernel,
        grid=(b,),
        in_specs=[pl.BlockSpec((1, c, d), lambda bi: (bi, 0, 0)), vec, vec, vec,
                  pl.BlockSpec((d, KV_WIDTH), lambda bi: (0, kblk)),
                  pl.BlockSpec((d, KV_WIDTH), lambda bi: (0, kblk + 1))],
        out_specs=[out, out],
        out_shape=[jax.ShapeDtypeStruct((b, c, KV_WIDTH), BF16)] * 2,
        compiler_params=_compiler_params(("arbitrary",)),
        name="ctx_kv",
    )(ctx, norm_g.reshape(1, d), scale_c, shift_c, w_in, w_in)


ATTN_TQ = 2048
Q_ROWS = GQA_GROUP * WINDOW


def _head_cols(h):
    return slice(h * HEAD_DIM, (h + 1) * HEAD_DIM)


def _attn_kernel(sink_ref, q_ref, k_ref, kp_ref, kn_ref, v_ref, vp_ref, vn_ref, kc_ref, vc_ref,
                 o_ref, kext, vaug, vcaug, s0, s1, p0, p1, d0, d1):
    i = pl.program_id(1)
    nb = ATTN_TQ // WINDOW
    nblk_total = pl.num_programs(1) * nb
    n_ctx = kc_ref.shape[1]
    s_scr, p_scr, d_scr = (s0, s1), (p0, p1), (d0, d1)

    kext[0:WINDOW] = kp_ref[0]
    kext[WINDOW:WINDOW + ATTN_TQ] = k_ref[0]
    kext[WINDOW + ATTN_TQ:] = kn_ref[0]
    ones = jnp.ones((ATTN_TQ + 2 * WINDOW, HEAD_DIM), BF16)
    for kh in range(N_KV_HEADS):
        vaug[kh, 0:WINDOW, 0:HEAD_DIM] = vp_ref[0, :, _head_cols(kh)]
        vaug[kh, WINDOW:WINDOW + ATTN_TQ, 0:HEAD_DIM] = v_ref[0, :, _head_cols(kh)]
        vaug[kh, WINDOW + ATTN_TQ:, 0:HEAD_DIM] = vn_ref[0, :, _head_cols(kh)]
        vaug[kh, :, HEAD_DIM:] = ones
        vcaug[kh, :, 0:HEAD_DIM] = vc_ref[0, :, _head_cols(kh)]
        vcaug[kh, :, HEAD_DIM:] = ones[0:n_ctx]

    r = lax.broadcasted_iota(jnp.int32, (Q_ROWS, WINDOW), 0) % WINDOW
    c = lax.broadcasted_iota(jnp.int32, (Q_ROWS, WINDOW), 1)
    neg = jnp.float32(-jnp.inf)
    bias_prev = jnp.where(c >= r, 0.0, neg).astype(F32)
    bias_next = jnp.where(c <= r, 0.0, neg).astype(F32)
    nt = (((1,), (1,)), ((), ()))

    def scores(j, kh, s_ref):
        base = pl.multiple_of(j * WINDOW, WINDOW)
        n = i * nb + j
        q3 = jnp.concatenate([q_ref[0, pl.ds(base, WINDOW), _head_cols(kh * GQA_GROUP + g)]
                              for g in range(GQA_GROUP)], axis=0)
        s_ctx = lax.dot_general(q3, kc_ref[0, :, _head_cols(kh)], nt, preferred_element_type=F32)
        s_win = lax.dot_general(q3, kext[pl.ds(base, 3 * WINDOW), _head_cols(kh)], nt,
                                preferred_element_type=F32)
        s_ref[:, 0:n_ctx] = s_ctx
        s_ref[:, n_ctx:n_ctx + WINDOW] = s_win[:, 0:WINDOW] + jnp.where(n > 0, bias_prev, neg)
        s_ref[:, n_ctx + WINDOW:n_ctx + 2 * WINDOW] = s_win[:, WINDOW:2 * WINDOW]
        s_ref[:, n_ctx + 2 * WINDOW:] = s_win[:, 2 * WINDOW:] + jnp.where(n < nblk_total - 1, bias_next, neg)

    def softmax(kh, s_ref, p_ref, d_ref):
        ntile = s_ref.shape[1] // LANES
        tiles = [s_ref[:, t * LANES:(t + 1) * LANES] for t in range(ntile)]
        mx = tiles[0]
        for t in range(1, ntile):
            mx = jnp.maximum(mx, tiles[t])
        sink = jnp.concatenate([jnp.full((WINDOW, 1), sink_ref[kh * GQA_GROUP + g] * math.log2(math.e), F32)
                                for g in range(GQA_GROUP)], axis=0)
        m = jnp.maximum(jnp.max(mx, axis=1, keepdims=True), sink)
        for t in range(ntile):
            p_ref[:, t * LANES:(t + 1) * LANES] = jnp.exp2(tiles[t] - m).astype(BF16)
        d_ref[...] = jnp.broadcast_to(jnp.exp2(sink - m), (Q_ROWS, HEAD_DIM))

    def values(j, kh, p_ref, d_ref):
        base = pl.multiple_of(j * WINDOW, WINDOW)
        o = jnp.dot(p_ref[:, 0:n_ctx], vcaug[kh], preferred_element_type=F32)
        o = o + jnp.dot(p_ref[:, n_ctx:], vaug[kh, pl.ds(base, 3 * WINDOW), :], preferred_element_type=F32)
        res = o[:, 0:HEAD_DIM] * (1.0 / (o[:, HEAD_DIM:] + d_ref[...]))
        for g in range(GQA_GROUP):
            o_ref[0, pl.ds(base, WINDOW), _head_cols(kh * GQA_GROUP + g)] = (
                res[g * WINDOW:(g + 1) * WINDOW].astype(o_ref.dtype))

    def tick(j, kh, do_scores=True, do_softmax=True, do_values=True):
        par = kh % 2
        if do_scores:
            scores(j, kh, s_scr[par])
        if do_softmax:
            khb = (kh - 1) % N_KV_HEADS
            softmax(khb, s_scr[1 - par], p_scr[1 - par], d_scr[1 - par])
        if do_values:
            jc, khc = (j, kh - 2) if kh >= 2 else (j - 1, kh + 2)
            values(jc, khc, p_scr[par], d_scr[par])

    tick(0, 0, do_softmax=False, do_values=False)
    tick(0, 1, do_values=False)
    tick(0, 2)
    tick(0, 3)

    def block(j, carry):
        for kh in range(N_KV_HEADS):
            tick(j, kh)
        return carry

    lax.fori_loop(1, nb, block, 0)
    tick(nb, 0, do_scores=False)
    tick(nb, 1, do_scores=False, do_softmax=False)


def _attention(q, k, v, kc, vc, sink):
    b, s, _ = q.shape
    c = kc.shape[1]
    steps = s // ATTN_TQ
    per = ATTN_TQ // WINDOW
    nblk = s // WINDOW
    main = lambda width: pl.BlockSpec((1, ATTN_TQ, width), lambda bi, i: (bi, i, 0))
    prev = pl.BlockSpec((1, WINDOW, KV_WIDTH), lambda bi, i: (bi, jnp.maximum(i * per - 1, 0), 0))
    nxt = pl.BlockSpec((1, WINDOW, KV_WIDTH), lambda bi, i: (bi, jnp.minimum((i + 1) * per, nblk - 1), 0))
    ctx = pl.BlockSpec((1, c, KV_WIDTH), lambda bi, i: (bi, 0, 0))
    return pl.pallas_call(
        _attn_kernel,
        grid=(b, steps),
        in_specs=[pl.BlockSpec(memory_space=pltpu.SMEM), main(ATTN_WIDTH),
                  main(KV_WIDTH), prev, nxt, main(KV_WIDTH), prev, nxt, ctx, ctx],
        out_specs=main(ATTN_WIDTH),
        out_shape=jax.ShapeDtypeStruct((b, s, ATTN_WIDTH), BF16),
        scratch_shapes=[pltpu.VMEM((ATTN_TQ + 2 * WINDOW, KV_WIDTH), BF16),
                        pltpu.VMEM((N_KV_HEADS, ATTN_TQ + 2 * WINDOW, 2 * HEAD_DIM), BF16),
                        pltpu.VMEM((N_KV_HEADS, c, 2 * HEAD_DIM), BF16)]
                       + [pltpu.VMEM((Q_ROWS, c + 3 * WINDOW), F32)] * 2
                       + [pltpu.VMEM((Q_ROWS, c + 3 * WINDOW), BF16)] * 2
                       + [pltpu.VMEM((Q_ROWS, HEAD_DIM), F32)] * 2,
        compiler_params=_compiler_params(("parallel", "parallel")),
        name="window_attn",
    )(sink, q, k, k, k, v, v, v, kc, vc)


FFT_HALF = SUBLANES
FFT_STEP = 2 * FFT_HALF
FFT_STEPS = FFT_RADIX // FFT_STEP


def _fourier_kernel(xa_ref, xb_ref, s1_ref, s2_ref, chan_ref, wf_ref, tw_ref, o_ref,
                    hr_scr, hi_scr, w_scr, yra, yia, yrb, yib, za, zb):
    r = FFT_RADIX
    t = pl.program_id(2)

    @pl.when(t == 0)
    def _():
        w = jnp.dot(chan_ref[...], wf_ref[0].astype(BF16), preferred_element_type=F32)
        w_scr[:, 0:r] = w[0:r].astype(BF16)
        w_scr[:, r:2 * r] = w[r:2 * r].astype(BF16)

    halves = ((xa_ref, yra, yia, za), (xb_ref, yrb, yib, zb))

    @pl.when(t < FFT_STEPS)
    def _():
        for x_ref, yr, yi, _ in halves:
            x2 = x_ref[...].reshape(r * FFT_HALF, LANES).astype(BF16)
            y = jnp.dot(x2, w_scr[...], preferred_element_type=F32)
            yr[...] = y[:, 0:r]
            yi[...] = y[:, r:2 * r]
        hs = []
        for _, yr, yi, _ in halves:
            top = jnp.concatenate([yr[pl.ds(j, r, stride=FFT_HALF), :].astype(BF16) for j in range(FFT_HALF)], axis=1)
            bot = jnp.concatenate([yi[pl.ds(j, r, stride=FFT_HALF), :].astype(BF16) for j in range(FFT_HALF)], axis=1)
            hs.append(jnp.dot(s1_ref[...], jnp.concatenate([top, bot], axis=0), preferred_element_type=F32))
        for half, h in enumerate(hs):
            for j in range(FFT_HALF):
                n1 = (2 * t + half) * FFT_HALF + j
                row = pl.multiple_of(n1 * FFT_PITCH, SUBLANES)
                hr_scr[pl.ds(row, r), :] = h[0:r, j * LANES:(j + 1) * LANES]
                hi_scr[pl.ds(row, r), :] = h[r:2 * r, j * LANES:(j + 1) * LANES]

    @pl.when(t >= FFT_STEPS)
    def _():
        zs = []
        for half in range(2):
            tops, bots = [], []
            for j in range(half * FFT_HALF, (half + 1) * FFT_HALF):
                v = (t - FFT_STEPS) * FFT_STEP + j
                hr = hr_scr[pl.ds(v, r, stride=FFT_PITCH), :]
                hi = hi_scr[pl.ds(v, r, stride=FFT_PITCH), :]
                tc = tw_ref[0, v].astype(F32)
                ts = tw_ref[1, v].astype(F32)
                tops.append((hr * tc + hi * ts).astype(BF16))
                bots.append((hi * tc - hr * ts).astype(BF16))
            rhs = jnp.concatenate([jnp.concatenate(tops, axis=1), jnp.concatenate(bots, axis=1)], axis=0)
            zs.append(jnp.dot(s2_ref[...], rhs, preferred_element_type=F32))
        for half, z in enumerate(zs):
            z_scr = halves[half][3]
            for j in range(FFT_HALF):
                z_scr[pl.ds(j, r, stride=FFT_HALF), :] = z[:, j * LANES:(j + 1) * LANES]
            o_ref[:, half * FFT_HALF:(half + 1) * FFT_HALF, :] = z_scr[...].reshape(r, FFT_HALF, LANES)


def _fourier_mix(f, w_f):
    b, s, _ = f.shape
    r = FFT_RADIX
    assert s == r * r
    s1, s2, chan, tw = _dft_tables()
    f5 = f.reshape(b, r, r // FFT_HALF, FFT_HALF, FOURIER_WIDTH)
    last = r // FFT_HALF - 1
    x_spec = lambda half: pl.BlockSpec(
        (None, r, None, FFT_HALF, LANES), lambda bi, g, t: (bi, 0, jnp.minimum(2 * t + half, last), 0, g))
    o_spec = pl.BlockSpec((None, r, None, FFT_STEP, LANES),
                          lambda bi, g, t: (bi, 0, jnp.maximum(t - FFT_STEPS, 0), 0, g))
    const = lambda shape: _resident(shape, lambda bi, g, t: (0,) * len(shape))
    z5 = pl.pallas_call(
        _fourier_kernel,
        grid=(b, FOURIER_GROUPS, 2 * FFT_STEPS),
        in_specs=[x_spec(0), x_spec(1), const((2 * r, 2 * r)), const((r, 2 * r)), const((2 * r, r)),
                  pl.BlockSpec((1, r, r), lambda bi, g, t: (g, 0, 0)), const((2, r, r, LANES))],
        out_specs=o_spec,
        out_shape=jax.ShapeDtypeStruct((b, r, FFT_STEPS, FFT_STEP, FOURIER_WIDTH), F32),
        scratch_shapes=[pltpu.VMEM((r * FFT_PITCH, LANES), F32), pltpu.VMEM((r * FFT_PITCH, LANES), F32),
                        pltpu.VMEM((r, 2 * r), BF16)]
                       + [pltpu.VMEM((r * FFT_HALF, LANES), F32)] * 6,
        compiler_params=_compiler_params(("parallel", "parallel", "arbitrary")),
        name="fourier_mix",
    )(f5, f5, *(jnp.asarray(a).astype(BF16) for a in (s1, s2, chan)), w_f, jnp.asarray(tw).astype(BF16))
    return z5.reshape(b, s, FOURIER_WIDTH)


def _outproj0_kernel(a_ref, zf_ref, sz_ref, x_ref, gate_ref, w_ref, o_ref):
    ya = a_ref[0] * sz_ref[0, :, 0:ATTN_WIDTH]
    yf = zf_ref[0].astype(BF16) * sz_ref[0, :, ATTN_WIDTH:]
    acc = jnp.dot(ya, w_ref[0:ATTN_WIDTH, :], preferred_element_type=F32)
    acc = acc + jnp.dot(yf, w_ref[ATTN_WIDTH:, :], preferred_element_type=F32)
    o_ref[0] = x_ref[0] + gate_ref[0] * acc


def _outproj0(attn, zf, sz, x, gate, w_out, tm=512):
    b, s, d = x.shape
    tok = lambda width: pl.BlockSpec((1, tm, width), lambda bi, i: (bi, i, 0))
    return pl.pallas_call(
        _outproj0_kernel,
        grid=(b, s // tm),
        in_specs=[tok(ATTN_WIDTH), tok(FOURIER_WIDTH), tok(d), tok(d),
                  pl.BlockSpec((1, 1, d), lambda bi, i: (bi, 0, 0)),
                  _resident((d, d), lambda bi, i: (0, 0))],
        out_specs=tok(d),
        out_shape=jax.ShapeDtypeStruct((b, s, d), F32),
        compiler_params=_compiler_params(("parallel", "parallel")),
        name="outproj0",
    )(attn, zf, sz, x, gate, w_out)


IN1_CHUNK = 512


def _inproj1_kernel(x_ref, g_ref, scale_ref, shift_ref, w_ref, u_ref, sz_ref, h_scr):
    tm = x_ref.shape[1]
    row_sets = [slice(r0, r0 + tm // ROW_SPLIT) for r0 in range(0, tm, tm // ROW_SPLIT)]

    def normalize(rows):
        h_scr[rows, :] = _norm_mod(x_ref[0, rows, :], g_ref[...], scale_ref[0], shift_ref[0]).astype(BF16)

    normalize(row_sets[0])
    for c0 in range(0, 2 * D_MODEL, IN1_CHUNK):
        for si, rows in enumerate(row_sets):
            acc = jnp.dot(h_scr[rows, :], w_ref[:, c0:c0 + IN1_CHUNK], preferred_element_type=F32)
            if c0 == 0 and si + 1 < len(row_sets):
                normalize(row_sets[si + 1])
            if c0 < D_MODEL:
                u_ref[0, rows, c0:c0 + IN1_CHUNK] = acc.astype(u_ref.dtype)
            else:
                sz_ref[0, rows, c0 - D_MODEL:c0 - D_MODEL + IN1_CHUNK] = _silu(acc).astype(sz_ref.dtype)


def _inproj1(x, norm_g, scale, shift, w_in, tm=512):
    b, s, d = x.shape
    vec = pl.BlockSpec((1, 1, d), lambda bi, i: (bi, 0, 0))
    tok = pl.BlockSpec((1, tm, d), lambda bi, i: (bi, i, 0))
    return pl.pallas_call(
        _inproj1_kernel,
        grid=(b, s // tm),
        in_specs=[tok, _resident((1, d), lambda bi, i: (0, 0)), vec, vec,
                  _resident((d, 2 * d), lambda bi, i: (0, 0))],
        out_specs=[tok, tok],
        out_shape=[jax.ShapeDtypeStruct((b, s, d), BF16)] * 2,
        scratch_shapes=[pltpu.VMEM((tm, d), BF16)],
        compiler_params=_compiler_params(("parallel", "parallel")),
        name="inproj1",
    )(x, norm_g.reshape(1, d), scale, shift, w_in)


POOL_HALO = BF16_SUBLANES
POOL_SUB = 128
POOL_PAD = 64


@functools.lru_cache(maxsize=None)
def _pool_band():
    p = np.arange(POOL_SUB)[:, None]
    e = np.arange(POOL_SUB + 2 * POOL_PAD)[None, :] - POOL_PAD
    return np.stack([((e >= p - w // 2) & (e <= p - w // 2 + w - 1)) for w in POOL_WINDOWS]).astype(np.float32)


def _pool_out_kernel(u_ref, up_ref, un_ref, sz_ref, x_ref, gate_ref, ls_ref, fn_ref,
                     inv_first_ref, inv_last_ref, inv_mid_ref, band_ref, wg_ref, wo_ref, o_ref, ext_scr, y_scr):
    i = pl.program_id(1)
    last = pl.num_programs(1) - 1
    tm = u_ref.shape[1]
    pad = POOL_PAD
    zeros = jnp.zeros((pad - POOL_HALO, u_ref.shape[2]), BF16)
    ext_scr[0:pad - POOL_HALO] = zeros
    ext_scr[pad - POOL_HALO:pad] = jnp.where(i > 0, up_ref[0], jnp.zeros_like(up_ref[0]))
    ext_scr[pad:pad + tm] = u_ref[0]
    ext_scr[pad + tm:pad + tm + POOL_HALO] = jnp.where(i < last, un_ref[0], jnp.zeros_like(un_ref[0]))
    ext_scr[pad + tm + POOL_HALO:] = zeros
    for g, w in enumerate(POOL_WINDOWS):
        cols = slice(g * POOL_GROUP_DIM, (g + 1) * POOL_GROUP_DIM)
        uc = u_ref[0, :, cols].astype(F32)
        wsum = jnp.concatenate(
            [jnp.dot(band_ref[g], ext_scr[r0:r0 + POOL_SUB + 2 * pad, cols], preferred_element_type=F32)
             for r0 in range(0, tm, POOL_SUB)], axis=0)
        inv_mid = inv_mid_ref[:, cols]
        inv_head = jnp.where(i == 0, inv_first_ref[:, cols], inv_mid)
        inv_tail = jnp.where(i == last, inv_last_ref[:, cols], inv_mid)
        mean = jnp.concatenate([wsum[0:SUBLANES] * inv_head,
                                wsum[SUBLANES:tm - SUBLANES] * inv_mid,
                                wsum[tm - SUBLANES:] * inv_tail], axis=0)
        pooled = (mean - uc).astype(BF16)
        y_scr[:, cols] = jnp.dot(pooled, wg_ref[g], preferred_element_type=F32)

    yz = (y_scr[...] * ls_ref[...]).astype(BF16) * sz_ref[0]
    acc = jnp.dot(yz, wo_ref[...], preferred_element_type=F32)
    x2 = x_ref[0] + gate_ref[0] * acc
    ms = jnp.mean(x2 * x2, axis=-1, keepdims=True)
    o_ref[0] = x2 * lax.rsqrt(ms + RMS_EPS) * fn_ref[...]


def _pool_out(u, sz, x, gate, layer_scale, final_norm, w_grp, w_out, tm=512):
    b, s, d = x.shape
    per = tm // POOL_HALO
    nh = s // POOL_HALO
    inv_first, inv_last, inv_mid = _pool_edge_tables(s)
    tok = pl.BlockSpec((1, tm, d), lambda bi, i: (bi, i, 0))
    prev = pl.BlockSpec((1, POOL_HALO, d), lambda bi, i: (bi, jnp.maximum(i * per - 1, 0), 0))
    nxt = pl.BlockSpec((1, POOL_HALO, d), lambda bi, i: (bi, jnp.minimum((i + 1) * per, nh - 1), 0))
    row = lambda rows: _resident((rows, d), lambda bi, i: (0, 0))
    return pl.pallas_call(
        _pool_out_kernel,
        grid=(b, s // tm),
        in_specs=[tok, prev, nxt, tok, tok, pl.BlockSpec((1, 1, d), lambda bi, i: (bi, 0, 0)),
                  row(1), row(1), row(SUBLANES), row(SUBLANES), row(1),
                  _resident((POOL_GROUPS, POOL_SUB, POOL_SUB + 2 * POOL_PAD), lambda bi, i: (0, 0, 0)),
                  _resident((POOL_GROUPS, POOL_GROUP_DIM, POOL_GROUP_DIM), lambda bi, i: (0, 0, 0)),
                  _resident((d, d), lambda bi, i: (0, 0))],
        out_specs=tok,
        out_shape=jax.ShapeDtypeStruct((b, s, d), F32),
        scratch_shapes=[pltpu.VMEM((tm + 2 * POOL_PAD, d), BF16), pltpu.VMEM((tm, d), F32)],
        compiler_params=_compiler_params(("parallel", "parallel")),
        name="pool_out",
    )(u, u, u, sz, x, gate, layer_scale.reshape(1, d), final_norm.reshape(1, d),
      jnp.asarray(inv_first), jnp.asarray(inv_last), jnp.asarray(inv_mid), jnp.asarray(_pool_band()).astype(BF16),
      w_grp, w_out)


def kernel(x, c, ctx, c_ctx, l0_norm, l0_w_mod, l0_b_mod, l0_w_in, l0_sink, l0_w_f, l0_w_out,
           l1_norm, l1_w_mod, l1_b_mod, l1_w_in, l1_w_grp, l1_scale, l1_w_out, final_norm):
    b, s, d = x.shape
    cond = jnp.concatenate([c, c_ctx[None, :], jnp.zeros((COND_ROWS - b - 1, d), F32)], axis=0)
    m0 = _modulation(cond, l0_w_mod, l0_b_mod)
    m1 = _modulation(cond, l1_w_mod, l1_b_mod)
    tok_vec = lambda m, k: m[:b, k * d:(k + 1) * d].reshape(b, 1, d)
    shift0, scale0, gate0 = (tok_vec(m0, k) for k in range(3))
    shift1, scale1, gate1 = (tok_vec(m1, k) for k in range(3))
    shift_c, scale_c = m0[b:b + 1, 0:d], m0[b:b + 1, d:2 * d]

    w_in0 = l0_w_in.astype(BF16)
    cos, sin = _rope_tables(s)
    q, k, v, f, sz0 = _inproj0(x, l0_norm, scale0, shift0, w_in0, jnp.asarray(cos), jnp.asarray(sin))
    kc, vc = _ctx_kv(ctx, l0_norm, scale_c, shift_c, w_in0)
    attn = _attention(q, k, v, kc, vc, l0_sink)
    zf = _fourier_mix(f, l0_w_f)
    x1 = _outproj0(attn, zf, sz0, x, gate0, l0_w_out.astype(BF16))

    u, sz1 = _inproj1(x1, l1_norm, scale1, shift1, l1_w_in.astype(BF16))
    return _pool_out(u, sz1, x1, gate1, l1_scale, final_norm, l1_w_grp.astype(BF16), l1_w_out.astype(BF16))
```

```python
import functools
import math

import numpy as np
import jax
import jax.numpy as jnp
from jax import lax
from jax.experimental import pallas as pl
from jax.experimental.pallas import tpu as pltpu

F32 = jnp.float32
BF16 = jnp.bfloat16

D_MODEL = 2048
GRID_W = 64
HEAD_DIM = 128
N_Q_HEADS = 12
N_KV_HEADS = 4
GQA_GROUP = N_Q_HEADS // N_KV_HEADS
WINDOW = 128
ATTN_WIDTH = N_Q_HEADS * HEAD_DIM
KV_WIDTH = N_KV_HEADS * HEAD_DIM
FOURIER_WIDTH = D_MODEL - ATTN_WIDTH
FOURIER_GROUPS = 4
FOURIER_GROUP_DIM = FOURIER_WIDTH // FOURIER_GROUPS
IN0_WIDTH = ATTN_WIDTH + 2 * KV_WIDTH + FOURIER_WIDTH + D_MODEL
POOL_WINDOWS = (2, 4, 8, 16)
POOL_GROUPS = len(POOL_WINDOWS)
POOL_GROUP_DIM = D_MODEL // POOL_GROUPS
ROPE_BASE = 10000.0
ROPE_PAIRS = HEAD_DIM // 4
RMS_EPS = 1e-6

LANES = 128
SUBLANES = 8
BF16_SUBLANES = 16
VMEM_LIMIT_BYTES = 56 * 1024 * 1024

FFT_RADIX = 128
FFT_PITCH = FFT_RADIX + SUBLANES

COND_ROWS = 8


ROW_SPLIT = 2


def _silu(x):
    hx = 0.5 * x
    return hx + hx * jnp.tanh(hx)


def _compiler_params(semantics):
    return pltpu.CompilerParams(dimension_semantics=semantics, vmem_limit_bytes=VMEM_LIMIT_BYTES)


def _resident(block_shape, index_map):
    return pl.BlockSpec(block_shape, index_map, pipeline_mode=pl.Buffered(1))


@functools.lru_cache(maxsize=None)
def _rope_tables(seq):
    pos = np.arange(seq)
    row = (pos // GRID_W).astype(np.float64)
    col = (pos % GRID_W).astype(np.float64)
    inv = 1.0 / (ROPE_BASE ** (np.arange(ROPE_PAIRS, dtype=np.float64) / ROPE_PAIRS))
    ar, ac = row[:, None] * inv, col[:, None] * inv
    cos = np.concatenate([np.cos(ar), np.cos(ar), np.cos(ac), np.cos(ac)], axis=1)
    sin = np.concatenate([-np.sin(ar), np.sin(ar), -np.sin(ac), np.sin(ac)], axis=1)
    return cos.astype(np.float32), sin.astype(np.float32)


@functools.lru_cache(maxsize=None)
def _dft_tables():
    r = FFT_RADIX
    idx = np.arange(r)
    ang = 2.0 * np.pi * np.outer(idx, idx) / r
    c, s = np.cos(ang) / math.sqrt(r), np.sin(ang) / math.sqrt(r)
    stage1 = np.block([[c, s], [-s, c]])
    stage2 = np.concatenate([c, s], axis=1)
    chan = np.concatenate([c, -s], axis=0)
    ang2 = 2.0 * np.pi * ((idx[:, None] * idx[None, :]) % (r * r)) / (r * r)
    tw = np.stack([np.cos(ang2), np.sin(ang2)])[..., None] * np.ones(LANES)
    return (stage1.astype(np.float32), stage2.astype(np.float32), chan.astype(np.float32),
            tw.astype(np.float32))


@functools.lru_cache(maxsize=None)
def _pool_edge_tables(seq):
    first = np.zeros((SUBLANES, D_MODEL), np.float64)
    last = np.zeros((SUBLANES, D_MODEL), np.float64)
    for g, w in enumerate(POOL_WINDOWS):
        for r in range(SUBLANES):
            for tab, t in ((first, r), (last, seq - SUBLANES + r)):
                lo = min(max(t - w // 2, 0), seq)
                hi = min(max(t - w // 2 + w, 0), seq)
                tab[r, g * POOL_GROUP_DIM:(g + 1) * POOL_GROUP_DIM] = 1.0 / (hi - lo)
    interior = np.repeat(1.0 / np.array(POOL_WINDOWS, np.float64), POOL_GROUP_DIM)[None, :]
    return first.astype(np.float32), last.astype(np.float32), interior.astype(np.float32)


def _mod_kernel(cond_ref, w_ref, b_ref, o_ref):
    a = _silu(cond_ref[...]).astype(BF16)
    o_ref[...] = jnp.dot(a, w_ref[...].astype(BF16), preferred_element_type=F32) + b_ref[...]


def _modulation(cond, w_mod, b_mod):
    d, n = w_mod.shape
    tn = 1536
    return pl.pallas_call(
        _mod_kernel,
        grid=(n // tn,),
        in_specs=[
            pl.BlockSpec((COND_ROWS, d), lambda j: (0, 0)),
            pl.BlockSpec((d, tn), lambda j: (0, j)),
            pl.BlockSpec((1, tn), lambda j: (0, j)),
        ],
        out_specs=pl.BlockSpec((COND_ROWS, tn), lambda j: (0, j)),
        out_shape=jax.ShapeDtypeStruct((COND_ROWS, n), F32),
        compiler_params=_compiler_params(("arbitrary",)),
        name="modulation",
    )(cond, w_mod, b_mod.reshape(1, n))


def _norm_mod(x, g, scale, shift):
    ms = jnp.mean(x * x, axis=-1, keepdims=True)
    return x * lax.rsqrt(ms + RMS_EPS) * g * (1.0 + scale) + shift


IN0_CHUNK = 512


def _rope(x, cos, sin, partner_is_right):
    rot = jnp.where(partner_is_right, pltpu.roll(x, HEAD_DIM - ROPE_PAIRS, 1), pltpu.roll(x, ROPE_PAIRS, 1))
    return x * cos + rot * sin


def _inproj0_kernel(x_ref, g_ref, scale_ref, shift_ref, w_ref, cos_ref, sin_ref,
                    q_ref, k_ref, v_ref, f_ref, sz_ref, h_scr):
    tm = x_ref.shape[1]
    q_scale = HEAD_DIM ** -0.5 * math.log2(math.e)
    lane = lax.broadcasted_iota(jnp.int32, (tm // ROW_SPLIT, HEAD_DIM), 1)
    partner_is_right = (lane % (2 * ROPE_PAIRS)) < ROPE_PAIRS
    heads_per_chunk = IN0_CHUNK // HEAD_DIM
    row_sets = [slice(r0, r0 + tm // ROW_SPLIT) for r0 in range(0, tm, tm // ROW_SPLIT)]

    def normalize(rows):
        h_scr[rows, :] = _norm_mod(x_ref[0, rows, :], g_ref[...], scale_ref[0], shift_ref[0]).astype(BF16)

    normalize(row_sets[0])
    for c0 in range(0, IN0_WIDTH, IN0_CHUNK):
        for si, rows in enumerate(row_sets):
            acc = jnp.dot(h_scr[rows, :], w_ref[:, c0:c0 + IN0_CHUNK], preferred_element_type=F32)
            if c0 == 0 and si + 1 < len(row_sets):
                normalize(row_sets[si + 1])
            if c0 < ATTN_WIDTH + KV_WIDTH:
                is_q = c0 < ATTN_WIDTH
                dst, off = (q_ref, c0) if is_q else (k_ref, c0 - ATTN_WIDTH)
                cs, sn = cos_ref[rows, :], sin_ref[rows, :]
                if is_q:
                    cs, sn = cs * q_scale, sn * q_scale
                for j in range(heads_per_chunk):
                    xh = acc[:, j * HEAD_DIM:(j + 1) * HEAD_DIM]
                    lo = off + j * HEAD_DIM
                    dst[0, rows, lo:lo + HEAD_DIM] = _rope(xh, cs, sn, partner_is_right).astype(dst.dtype)
            elif c0 < ATTN_WIDTH + 2 * KV_WIDTH:
                v_ref[0, rows, :] = acc.astype(v_ref.dtype)
            elif c0 < ATTN_WIDTH + 2 * KV_WIDTH + FOURIER_WIDTH:
                f_ref[0, rows, :] = acc
            else:
                off = c0 - (IN0_WIDTH - D_MODEL)
                sz_ref[0, rows, off:off + IN0_CHUNK] = _silu(acc).astype(sz_ref.dtype)


def _inproj0(x, norm_g, scale, shift, w_in, cos, sin, tm=512):
    b, s, d = x.shape
    vec = pl.BlockSpec((1, 1, d), lambda bi, i: (bi, 0, 0))
    tok = lambda width: pl.BlockSpec((1, tm, width), lambda bi, i: (bi, i, 0))
    tab = pl.BlockSpec((tm, HEAD_DIM), lambda bi, i: (i, 0))
    return pl.pallas_call(
        _inproj0_kernel,
        grid=(b, s // tm),
        in_specs=[tok(d), _resident((1, d), lambda bi, i: (0, 0)), vec, vec,
                  _resident((d, IN0_WIDTH), lambda bi, i: (0, 0)), tab, tab],
        out_specs=[tok(ATTN_WIDTH), tok(KV_WIDTH), tok(KV_WIDTH), tok(FOURIER_WIDTH), tok(D_MODEL)],
        out_shape=[
            jax.ShapeDtypeStruct((b, s, ATTN_WIDTH), BF16),
            jax.ShapeDtypeStruct((b, s, KV_WIDTH), BF16),
            jax.ShapeDtypeStruct((b, s, KV_WIDTH), BF16),
            jax.ShapeDtypeStruct((b, s, FOURIER_WIDTH), F32),
            jax.ShapeDtypeStruct((b, s, D_MODEL), BF16),
        ],
        scratch_shapes=[pltpu.VMEM((tm, d), BF16)],
        compiler_params=_compiler_params(("parallel", "parallel")),
        name="inproj0",
    )(x, norm_g.reshape(1, d), scale, shift, w_in, cos, sin)


def _ctx_kv_kernel(c_ref, g_ref, scale_ref, shift_ref, wk_ref, wv_ref, k_ref, v_ref):
    h = _norm_mod(c_ref[0], g_ref[...], scale_ref[...], shift_ref[...]).astype(BF16)
    k_ref[0] = jnp.dot(h, wk_ref[...], preferred_element_type=F32).astype(k_ref.dtype)
    v_ref[0] = jnp.dot(h, wv_ref[...], preferred_element_type=F32).astype(v_ref.dtype)


def _ctx_kv(ctx, norm_g, scale_c, shift_c, w_in):
    b, c, d = ctx.shape
    vec = pl.BlockSpec((1, d), lambda bi: (0, 0))
    kblk = ATTN_WIDTH // KV_WIDTH
    out = pl.BlockSpec((1, c, KV_WIDTH), lambda bi: (bi, 0, 0))
    return pl.pallas_call(
        _ctx_kv_kernel,
        grid=(b,),
        in_specs=[pl.BlockSpec((1, c, d), lambda bi: (bi, 0, 0)), vec, vec, vec,
                  pl.BlockSpec((d, KV_WIDTH), lambda bi: (0, kblk)),
                  pl.BlockSpec((d, KV_WIDTH), lambda bi: (0, kblk + 1))],
        out_specs=[out, out],
        out_shape=[jax.ShapeDtypeStruct((b, c, KV_WIDTH), BF16)] * 2,
        compiler_params=_compiler_params(("arbitrary",)),
        name="ctx_kv",
    )(ctx, norm_g.reshape(1, d), scale_c, shift_c, w_in, w_in)


ATTN_TQ = 2048
Q_ROWS = GQA_GROUP * WINDOW


def _head_cols(h):
    return slice(h * HEAD_DIM, (h + 1) * HEAD_DIM)


def _attn_kernel(sink_ref, q_ref, k_ref, kp_ref, kn_ref, v_ref, vp_ref, vn_ref, kc_ref, vc_ref,
                 o_ref, kext, vaug, vcaug, s0, s1, p0, p1, d0, d1):
    i = pl.program_id(1)
    nb = ATTN_TQ // WINDOW
    nblk_total = pl.num_programs(1) * nb
    n_ctx = kc_ref.shape[1]
    s_scr, p_scr, d_scr = (s0, s1), (p0, p1), (d0, d1)

    kext[0:WINDOW] = kp_ref[0]
    kext[WINDOW:WINDOW + ATTN_TQ] = k_ref[0]
    kext[WINDOW + ATTN_TQ:] = kn_ref[0]
    ones = jnp.ones((ATTN_TQ + 2 * WINDOW, HEAD_DIM), BF16)
    for kh in range(N_KV_HEADS):
        vaug[kh, 0:WINDOW, 0:HEAD_DIM] = vp_ref[0, :, _head_cols(kh)]
        vaug[kh, WINDOW:WINDOW + ATTN_TQ, 0:HEAD_DIM] = v_ref[0, :, _head_cols(kh)]
        vaug[kh, WINDOW + ATTN_TQ:, 0:HEAD_DIM] = vn_ref[0, :, _head_cols(kh)]
        vaug[kh, :, HEAD_DIM:] = ones
        vcaug[kh, :, 0:HEAD_DIM] = vc_ref[0, :, _head_cols(kh)]
        vcaug[kh, :, HEAD_DIM:] = ones[0:n_ctx]

    r = lax.broadcasted_iota(jnp.int32, (Q_ROWS, WINDOW), 0) % WINDOW
    c = lax.broadcasted_iota(jnp.int32, (Q_ROWS, WINDOW), 1)
    neg = jnp.float32(-jnp.inf)
    bias_prev = jnp.where(c >= r, 0.0, neg).astype(F32)
    bias_next = jnp.where(c <= r, 0.0, neg).astype(F32)
    nt = (((1,), (1,)), ((), ()))

    def scores(j, kh, s_ref):
        base = pl.multiple_of(j * WINDOW, WINDOW)
        n = i * nb + j
        q3 = jnp.concatenate([q_ref[0, pl.ds(base, WINDOW), _head_cols(kh * GQA_GROUP + g)]
                              for g in range(GQA_GROUP)], axis=0)
        s_ctx = lax.dot_general(q3, kc_ref[0, :, _head_cols(kh)], nt, preferred_element_type=F32)
        s_win = lax.dot_general(q3, kext[pl.ds(base, 3 * WINDOW), _head_cols(kh)], nt,
                                preferred_element_type=F32)
        s_ref[:, 0:n_ctx] = s_ctx
        s_ref[:, n_ctx:n_ctx + WINDOW] = s_win[:, 0:WINDOW] + jnp.where(n > 0, bias_prev, neg)
        s_ref[:, n_ctx + WINDOW:n_ctx + 2 * WINDOW] = s_win[:, WINDOW:2 * WINDOW]
        s_ref[:, n_ctx + 2 * WINDOW:] = s_win[:, 2 * WINDOW:] + jnp.where(n < nblk_total - 1, bias_next, neg)

    def softmax(kh, s_ref, p_ref, d_ref):
        ntile = s_ref.shape[1] // LANES
        tiles = [s_ref[:, t * LANES:(t + 1) * LANES] for t in range(ntile)]
        mx = tiles[0]
        for t in range(1, ntile):
            mx = jnp.maximum(mx, tiles[t])
        sink = jnp.concatenate([jnp.full((WINDOW, 1), sink_ref[kh * GQA_GROUP + g] * math.log2(math.e), F32)
                                for g in range(GQA_GROUP)], axis=0)
        m = jnp.maximum(jnp.max(mx, axis=1, keepdims=True), sink)
        for t in range(ntile):
            p_ref[:, t * LANES:(t + 1) * LANES] = jnp.exp2(tiles[t] - m).astype(BF16)
        d_ref[...] = jnp.broadcast_to(jnp.exp2(sink - m), (Q_ROWS, HEAD_DIM))

    def values(j, kh, p_ref, d_ref):
        base = pl.multiple_of(j * WINDOW, WINDOW)
        o = jnp.dot(p_ref[:, 0:n_ctx], vcaug[kh], preferred_element_type=F32)
        o = o + jnp.dot(p_ref[:, n_ctx:], vaug[kh, pl.ds(base, 3 * WINDOW), :], preferred_element_type=F32)
        res = o[:, 0:HEAD_DIM] * (1.0 / (o[:, HEAD_DIM:] + d_ref[...]))
        for g in range(GQA_GROUP):
            o_ref[0, pl.ds(base, WINDOW), _head_cols(kh * GQA_GROUP + g)] = (
                res[g * WINDOW:(g + 1) * WINDOW].astype(o_ref.dtype))

    def tick(j, kh, do_scores=True, do_softmax=True, do_values=True):
        par = kh % 2
        if do_scores:
            scores(j, kh, s_scr[par])
        if do_softmax:
            khb = (kh - 1) % N_KV_HEADS
            softmax(khb, s_scr[1 - par], p_scr[1 - par], d_scr[1 - par])
        if do_values:
            jc, khc = (j, kh - 2) if kh >= 2 else (j - 1, kh + 2)
            values(jc, khc, p_scr[par], d_scr[par])

    tick(0, 0, do_softmax=False, do_values=False)
    tick(0, 1, do_values=False)
    tick(0, 2)
    tick(0, 3)

    def block(j, carry):
        for kh in range(N_KV_HEADS):
            tick(j, kh)
        return carry

    lax.fori_loop(1, nb, block, 0)
    tick(nb, 0, do_scores=False)
    tick(nb, 1, do_scores=False, do_softmax=False)


def _attention(q, k, v, kc, vc, sink):
    b, s, _ = q.shape
    c = kc.shape[1]
    steps = s // ATTN_TQ
    per = ATTN_TQ // WINDOW
    nblk = s // WINDOW
    main = lambda width: pl.BlockSpec((1, ATTN_TQ, width), lambda bi, i: (bi, i, 0))
    prev = pl.BlockSpec((1, WINDOW, KV_WIDTH), lambda bi, i: (bi, jnp.maximum(i * per - 1, 0), 0))
    nxt = pl.BlockSpec((1, WINDOW, KV_WIDTH), lambda bi, i: (bi, jnp.minimum((i + 1) * per, nblk - 1), 0))
    ctx = pl.BlockSpec((1, c, KV_WIDTH), lambda bi, i: (bi, 0, 0))
    return pl.pallas_call(
        _attn_kernel,
        grid=(b, steps),
        in_specs=[pl.BlockSpec(memory_space=pltpu.SMEM), main(ATTN_WIDTH),
                  main(KV_WIDTH), prev, nxt, main(KV_WIDTH), prev, nxt, ctx, ctx],
        out_specs=main(ATTN_WIDTH),
        out_shape=jax.ShapeDtypeStruct((b, s, ATTN_WIDTH), BF16),
        scratch_shapes=[pltpu.VMEM((ATTN_TQ + 2 * WINDOW, KV_WIDTH), BF16),
                        pltpu.VMEM((N_KV_HEADS, ATTN_TQ + 2 * WINDOW, 2 * HEAD_DIM), BF16),
                        pltpu.VMEM((N_KV_HEADS, c, 2 * HEAD_DIM), BF16)]
                       + [pltpu.VMEM((Q_ROWS, c + 3 * WINDOW), F32)] * 2
                       + [pltpu.VMEM((Q_ROWS, c + 3 * WINDOW), BF16)] * 2
                       + [pltpu.VMEM((Q_ROWS, HEAD_DIM), F32)] * 2,
        compiler_params=_compiler_params(("parallel", "parallel")),
        name="window_attn",
    )(sink, q, k, k, k, v, v, v, kc, vc)


FFT_SUB = SUBLANES
FFT_BLOCKS = 4
FFT_STEP = FFT_BLOCKS * FFT_SUB
FFT_STEPS = FFT_RADIX // FFT_STEP


def _fourier_kernel(*refs):
    nblk = FFT_BLOCKS
    x_refs = refs[:nblk]
    s1_ref, s2_ref, chan_ref, wf_ref, tw_ref, o_ref, hr_scr, hi_scr, w_scr = refs[nblk:nblk + 9]
    scr = refs[nblk + 9:]
    yr_scr, yi_scr, z_scr = scr[0:nblk], scr[nblk:2 * nblk], scr[2 * nblk:3 * nblk]
    r = FFT_RADIX
    t = pl.program_id(2)

    @pl.when(t == 0)
    def _():
        w = jnp.dot(chan_ref[...], wf_ref[0].astype(BF16), preferred_element_type=F32)
        w_scr[:, 0:r] = w[0:r].astype(BF16)
        w_scr[:, r:2 * r] = w[r:2 * r].astype(BF16)


    def channel_stage(i):
        x2 = x_refs[i][...].reshape(r * FFT_SUB, LANES).astype(BF16)
        y = jnp.dot(x2, w_scr[...], preferred_element_type=F32)
        yr_scr[i][...] = y[:, 0:r]
        yi_scr[i][...] = y[:, r:2 * r]

    def n2_stage(i):
        top = jnp.concatenate([yr_scr[i][pl.ds(j, r, stride=FFT_SUB), :].astype(BF16) for j in range(FFT_SUB)], axis=1)
        bot = jnp.concatenate([yi_scr[i][pl.ds(j, r, stride=FFT_SUB), :].astype(BF16) for j in range(FFT_SUB)], axis=1)
        h = jnp.dot(s1_ref[...], jnp.concatenate([top, bot], axis=0), preferred_element_type=F32)
        for j in range(FFT_SUB):
            n1 = (nblk * t + i) * FFT_SUB + j
            row = pl.multiple_of(n1 * FFT_PITCH, SUBLANES)
            hr_scr[pl.ds(row, r), :] = h[0:r, j * LANES:(j + 1) * LANES]
            hi_scr[pl.ds(row, r), :] = h[r:2 * r, j * LANES:(j + 1) * LANES]

    @pl.when(t < FFT_STEPS)
    def _():
        for i in range(nblk + 1):
            if i < nblk:
                channel_stage(i)
            if i > 0:
                n2_stage(i - 1)

    def twiddle_stage(i):
        tops, bots = [], []
        for j in range(i * FFT_SUB, (i + 1) * FFT_SUB):
            v = (t - FFT_STEPS) * FFT_STEP + j
            hr = hr_scr[pl.ds(v, r, stride=FFT_PITCH), :]
            hi = hi_scr[pl.ds(v, r, stride=FFT_PITCH), :]
            tc = tw_ref[0, v].astype(F32)
            ts = tw_ref[1, v].astype(F32)
            tops.append((hr * tc + hi * ts).astype(BF16))
            bots.append((hi * tc - hr * ts).astype(BF16))
        return jnp.concatenate([jnp.concatenate(tops, axis=1), jnp.concatenate(bots, axis=1)], axis=0)

    def n1_stage(i, rhs):
        z = jnp.dot(s2_ref[...], rhs, preferred_element_type=F32)
        for j in range(FFT_SUB):
            z_scr[i][pl.ds(j, r, stride=FFT_SUB), :] = z[:, j * LANES:(j + 1) * LANES]
        o_ref[:, i * FFT_SUB:(i + 1) * FFT_SUB, :] = z_scr[i][...].reshape(r, FFT_SUB, LANES)

    @pl.when(t >= FFT_STEPS)
    def _():
        rhs = [None] * nblk
        for i in range(nblk + 1):
            if i < nblk:
                rhs[i] = twiddle_stage(i)
            if i > 0:
                n1_stage(i - 1, rhs[i - 1])


def _fourier_mix(f, w_f):
    b, s, _ = f.shape
    r = FFT_RADIX
    assert s == r * r
    s1, s2, chan, tw = _dft_tables()
    f5 = f.reshape(b, r, r // FFT_SUB, FFT_SUB, FOURIER_WIDTH)
    last = r // FFT_SUB - 1
    x_spec = lambda i: pl.BlockSpec(
        (None, r, None, FFT_SUB, LANES), lambda bi, g, t: (bi, 0, jnp.minimum(FFT_BLOCKS * t + i, last), 0, g))
    o_spec = pl.BlockSpec((None, r, None, FFT_STEP, LANES),
                          lambda bi, g, t: (bi, 0, jnp.maximum(t - FFT_STEPS, 0), 0, g))
    const = lambda shape: _resident(shape, lambda bi, g, t: (0,) * len(shape))
    z5 = pl.pallas_call(
        _fourier_kernel,
        grid=(b, FOURIER_GROUPS, 2 * FFT_STEPS),
        in_specs=[x_spec(i) for i in range(FFT_BLOCKS)]
                 + [const((2 * r, 2 * r)), const((r, 2 * r)), const((2 * r, r)),
                    pl.BlockSpec((1, r, r), lambda bi, g, t: (g, 0, 0)), const((2, r, r, LANES))],
        out_specs=o_spec,
        out_shape=jax.ShapeDtypeStruct((b, r, FFT_STEPS, FFT_STEP, FOURIER_WIDTH), F32),
        scratch_shapes=[pltpu.VMEM((r * FFT_PITCH, LANES), F32), pltpu.VMEM((r * FFT_PITCH, LANES), F32),
                        pltpu.VMEM((r, 2 * r), BF16)]
                       + [pltpu.VMEM((r * FFT_SUB, LANES), F32)] * (3 * FFT_BLOCKS),
        compiler_params=_compiler_params(("parallel", "parallel", "arbitrary")),
        name="fourier_mix",
    )(*([f5] * FFT_BLOCKS), *(jnp.asarray(a).astype(BF16) for a in (s1, s2, chan)), w_f,
      jnp.asarray(tw).astype(BF16))
    return z5.reshape(b, s, FOURIER_WIDTH)


def _outproj0_kernel(a_ref, zf_ref, sz_ref, x_ref, gate_ref, w_ref, o_ref):
    ya = a_ref[0] * sz_ref[0, :, 0:ATTN_WIDTH]
    yf = zf_ref[0].astype(BF16) * sz_ref[0, :, ATTN_WIDTH:]
    acc = jnp.dot(ya, w_ref[0:ATTN_WIDTH, :], preferred_element_type=F32)
    acc = acc + jnp.dot(yf, w_ref[ATTN_WIDTH:, :], preferred_element_type=F32)
    o_ref[0] = x_ref[0] + gate_ref[0] * acc


def _outproj0(attn, zf, sz, x, gate, w_out, tm=512):
    b, s, d = x.shape
    tok = lambda width: pl.BlockSpec((1, tm, width), lambda bi, i: (bi, i, 0))
    return pl.pallas_call(
        _outproj0_kernel,
        grid=(b, s // tm),
        in_specs=[tok(ATTN_WIDTH), tok(FOURIER_WIDTH), tok(d), tok(d),
                  pl.BlockSpec((1, 1, d), lambda bi, i: (bi, 0, 0)),
                  _resident((d, d), lambda bi, i: (0, 0))],
        out_specs=tok(d),
        out_shape=jax.ShapeDtypeStruct((b, s, d), F32),
        compiler_params=_compiler_params(("parallel", "parallel")),
        name="outproj0",
    )(attn, zf, sz, x, gate, w_out)


IN1_CHUNK = 512


def _inproj1_kernel(x_ref, g_ref, scale_ref, shift_ref, w_ref, u_ref, sz_ref, h_scr):
    tm = x_ref.shape[1]
    row_sets = [slice(r0, r0 + tm // ROW_SPLIT) for r0 in range(0, tm, tm // ROW_SPLIT)]

    def normalize(rows):
        h_scr[rows, :] = _norm_mod(x_ref[0, rows, :], g_ref[...], scale_ref[0], shift_ref[0]).astype(BF16)

    normalize(row_sets[0])
    for c0 in range(0, 2 * D_MODEL, IN1_CHUNK):
        for si, rows in enumerate(row_sets):
            acc = jnp.dot(h_scr[rows, :], w_ref[:, c0:c0 + IN1_CHUNK], preferred_element_type=F32)
            if c0 == 0 and si + 1 < len(row_sets):
                normalize(row_sets[si + 1])
            if c0 < D_MODEL:
                u_ref[0, rows, c0:c0 + IN1_CHUNK] = acc.astype(u_ref.dtype)
            else:
                sz_ref[0, rows, c0 - D_MODEL:c0 - D_MODEL + IN1_CHUNK] = _silu(acc).astype(sz_ref.dtype)


def _inproj1(x, norm_g, scale, shift, w_in, tm=512):
    b, s, d = x.shape
    vec = pl.BlockSpec((1, 1, d), lambda bi, i: (bi, 0, 0))
    tok = pl.BlockSpec((1, tm, d), lambda bi, i: (bi, i, 0))
    return pl.pallas_call(
        _inproj1_kernel,
        grid=(b, s // tm),
        in_specs=[tok, _resident((1, d), lambda bi, i: (0, 0)), vec, vec,
                  _resident((d, 2 * d), lambda bi, i: (0, 0))],
        out_specs=[tok, tok],
        out_shape=[jax.ShapeDtypeStruct((b, s, d), BF16)] * 2,
        scratch_shapes=[pltpu.VMEM((tm, d), BF16)],
        compiler_params=_compiler_params(("parallel", "parallel")),
        name="inproj1",
    )(x, norm_g.reshape(1, d), scale, shift, w_in)


POOL_HALO = BF16_SUBLANES
POOL_SUB = 128
POOL_PAD = 64


@functools.lru_cache(maxsize=None)
def _pool_band():
    p = np.arange(POOL_SUB)[:, None]
    e = np.arange(POOL_SUB + 2 * POOL_PAD)[None, :] - POOL_PAD
    return np.stack([((e >= p - w // 2) & (e <= p - w // 2 + w - 1)) for w in POOL_WINDOWS]).astype(np.float32)


def _pool_out_kernel(u_ref, up_ref, un_ref, sz_ref, x_ref, gate_ref, ls_ref, fn_ref,
                     inv_first_ref, inv_last_ref, inv_mid_ref, band_ref, wg_ref, wo_ref, o_ref, ext_scr, y_scr):
    i = pl.program_id(1)
    last = pl.num_programs(1) - 1
    tm = u_ref.shape[1]
    pad = POOL_PAD
    zeros = jnp.zeros((pad - POOL_HALO, u_ref.shape[2]), BF16)
    ext_scr[0:pad - POOL_HALO] = zeros
    ext_scr[pad - POOL_HALO:pad] = jnp.where(i > 0, up_ref[0], jnp.zeros_like(up_ref[0]))
    ext_scr[pad:pad + tm] = u_ref[0]
    ext_scr[pad + tm:pad + tm + POOL_HALO] = jnp.where(i < last, un_ref[0], jnp.zeros_like(un_ref[0]))
    ext_scr[pad + tm + POOL_HALO:] = zeros
    for g, w in enumerate(POOL_WINDOWS):
        cols = slice(g * POOL_GROUP_DIM, (g + 1) * POOL_GROUP_DIM)
        uc = u_ref[0, :, cols].astype(F32)
        wsum = jnp.concatenate(
            [jnp.dot(band_ref[g], ext_scr[r0:r0 + POOL_SUB + 2 * pad, cols], preferred_element_type=F32)
             for r0 in range(0, tm, POOL_SUB)], axis=0)
        inv_mid = inv_mid_ref[:, cols]
        inv_head = jnp.where(i == 0, inv_first_ref[:, cols], inv_mid)
        inv_tail = jnp.where(i == last, inv_last_ref[:, cols], inv_mid)
        mean = jnp.concatenate([wsum[0:SUBLANES] * inv_head,
                                wsum[SUBLANES:tm - SUBLANES] * inv_mid,
                                wsum[tm - SUBLANES:] * inv_tail], axis=0)
        pooled = (mean - uc).astype(BF16)
        y_scr[:, cols] = jnp.dot(pooled, wg_ref[g], preferred_element_type=F32)

    yz = (y_scr[...] * ls_ref[...]).astype(BF16) * sz_ref[0]
    acc = jnp.dot(yz, wo_ref[...], preferred_element_type=F32)
    x2 = x_ref[0] + gate_ref[0] * acc
    ms = jnp.mean(x2 * x2, axis=-1, keepdims=True)
    o_ref[0] = x2 * lax.rsqrt(ms + RMS_EPS) * fn_ref[...]


def _pool_out(u, sz, x, gate, layer_scale, final_norm, w_grp, w_out, tm=512):
    b, s, d = x.shape
    per = tm // POOL_HALO
    nh = s // POOL_HALO
    inv_first, inv_last, inv_mid = _pool_edge_tables(s)
    tok = pl.BlockSpec((1, tm, d), lambda bi, i: (bi, i, 0))
    prev = pl.BlockSpec((1, POOL_HALO, d), lambda bi, i: (bi, jnp.maximum(i * per - 1, 0), 0))
    nxt = pl.BlockSpec((1, POOL_HALO, d), lambda bi, i: (bi, jnp.minimum((i + 1) * per, nh - 1), 0))
    row = lambda rows: _resident((rows, d), lambda bi, i: (0, 0))
    return pl.pallas_call(
        _pool_out_kernel,
        grid=(b, s // tm),
        in_specs=[tok, prev, nxt, tok, tok, pl.BlockSpec((1, 1, d), lambda bi, i: (bi, 0, 0)),
                  row(1), row(1), row(SUBLANES), row(SUBLANES), row(1),
                  _resident((POOL_GROUPS, POOL_SUB, POOL_SUB + 2 * POOL_PAD), lambda bi, i: (0, 0, 0)),
                  _resident((POOL_GROUPS, POOL_GROUP_DIM, POOL_GROUP_DIM), lambda bi, i: (0, 0, 0)),
                  _resident((d, d), lambda bi, i: (0, 0))],
        out_specs=tok,
        out_shape=jax.ShapeDtypeStruct((b, s, d), F32),
        scratch_shapes=[pltpu.VMEM((tm + 2 * POOL_PAD, d), BF16), pltpu.VMEM((tm, d), F32)],
        compiler_params=_compiler_params(("parallel", "parallel")),
        name="pool_out",
    )(u, u, u, sz, x, gate, layer_scale.reshape(1, d), final_norm.reshape(1, d),
      jnp.asarray(inv_first), jnp.asarray(inv_last), jnp.asarray(inv_mid), jnp.asarray(_pool_band()).astype(BF16),
      w_grp, w_out)


def kernel(x, c, ctx, c_ctx, l0_norm, l0_w_mod, l0_b_mod, l0_w_in, l0_sink, l0_w_f, l0_w_out,
           l1_norm, l1_w_mod, l1_b_mod, l1_w_in, l1_w_grp, l1_scale, l1_w_out, final_norm):
    b, s, d = x.shape
    cond = jnp.concatenate([c, c_ctx[None, :], jnp.zeros((COND_ROWS - b - 1, d), F32)], axis=0)
    m0 = _modulation(cond, l0_w_mod, l0_b_mod)
    m1 = _modulation(cond, l1_w_mod, l1_b_mod)
    tok_vec = lambda m, k: m[:b, k * d:(k + 1) * d].reshape(b, 1, d)
    shift0, scale0, gate0 = (tok_vec(m0, k) for k in range(3))
    shift1, scale1, gate1 = (tok_vec(m1, k) for k in range(3))
    shift_c, scale_c = m0[b:b + 1, 0:d], m0[b:b + 1, d:2 * d]

    w_in0 = l0_w_in.astype(BF16)
    cos, sin = _rope_tables(s)
    q, k, v, f, sz0 = _inproj0(x, l0_norm, scale0, shift0, w_in0, jnp.asarray(cos), jnp.asarray(sin))
    kc, vc = _ctx_kv(ctx, l0_norm, scale_c, shift_c, w_in0)
    attn = _attention(q, k, v, kc, vc, l0_sink)
    zf = _fourier_mix(f, l0_w_f)
    x1 = _outproj0(attn, zf, sz0, x, gate0, l0_w_out.astype(BF16))

    u, sz1 = _inproj1(x1, l1_norm, scale1, shift1, l1_w_in.astype(BF16))
    return _pool_out(u, sz1, x1, gate1, l1_scale, final_norm, l1_w_grp.astype(BF16), l1_w_out.astype(BF16))
```

```python
import functools
import math

import numpy as np
import jax
import jax.numpy as jnp
from jax import lax
from jax.experimental import pallas as pl
from jax.experimental.pallas import tpu as pltpu

F32 = jnp.float32
BF16 = jnp.bfloat16

D_MODEL = 2048
GRID_W = 64
HEAD_DIM = 128
N_Q_HEADS = 12
N_KV_HEADS = 4
GQA_GROUP = N_Q_HEADS // N_KV_HEADS
WINDOW = 128
ATTN_WIDTH = N_Q_HEADS * HEAD_DIM
KV_WIDTH = N_KV_HEADS * HEAD_DIM
FOURIER_WIDTH = D_MODEL - ATTN_WIDTH
FOURIER_GROUPS = 4
FOURIER_GROUP_DIM = FOURIER_WIDTH // FOURIER_GROUPS
IN0_WIDTH = ATTN_WIDTH + 2 * KV_WIDTH + FOURIER_WIDTH + D_MODEL
POOL_WINDOWS = (2, 4, 8, 16)
POOL_GROUPS = len(POOL_WINDOWS)
POOL_GROUP_DIM = D_MODEL // POOL_GROUPS
ROPE_BASE = 10000.0
ROPE_PAIRS = HEAD_DIM // 4
RMS_EPS = 1e-6

LANES = 128
SUBLANES = 8
BF16_SUBLANES = 16
VMEM_LIMIT_BYTES = 56 * 1024 * 1024

FFT_RADIX = 128
FFT_PITCH = FFT_RADIX + SUBLANES

COND_ROWS = 8


ROW_SPLIT = 2


def _silu(x):
    hx = 0.5 * x
    return hx + hx * jnp.tanh(hx)


def _compiler_params(semantics):
    return pltpu.CompilerParams(dimension_semantics=semantics, vmem_limit_bytes=VMEM_LIMIT_BYTES)


def _resident(block_shape, index_map):
    return pl.BlockSpec(block_shape, index_map, pipeline_mode=pl.Buffered(1))


@functools.lru_cache(maxsize=None)
def _rope_tables(seq):
    pos = np.arange(seq)
    row = (pos // GRID_W).astype(np.float64)
    col = (pos % GRID_W).astype(np.float64)
    inv = 1.0 / (ROPE_BASE ** (np.arange(ROPE_PAIRS, dtype=np.float64) / ROPE_PAIRS))
    ar, ac = row[:, None] * inv, col[:, None] * inv
    cos = np.concatenate([np.cos(ar), np.cos(ar), np.cos(ac), np.cos(ac)], axis=1)
    sin = np.concatenate([-np.sin(ar), np.sin(ar), -np.sin(ac), np.sin(ac)], axis=1)
    return cos.astype(np.float32), sin.astype(np.float32)


@functools.lru_cache(maxsize=None)
def _dft_tables():
    r = FFT_RADIX
    idx = np.arange(r)
    ang = 2.0 * np.pi * np.outer(idx, idx) / r
    c, s = np.cos(ang) / math.sqrt(r), np.sin(ang) / math.sqrt(r)
    stage1 = np.block([[c, s], [-s, c]])
    stage2 = np.concatenate([c, s], axis=1)
    chan = np.concatenate([c, -s], axis=0)
    ang2 = 2.0 * np.pi * ((idx[:, None] * idx[None, :]) % (r * r)) / (r * r)
    tw = np.stack([np.cos(ang2), np.sin(ang2)])[..., None] * np.ones(LANES)
    return (stage1.astype(np.float32), stage2.astype(np.float32), chan.astype(np.float32),
            tw.astype(np.float32))


@functools.lru_cache(maxsize=None)
def _pool_edge_tables(seq):
    first = np.zeros((SUBLANES, D_MODEL), np.float64)
    last = np.zeros((SUBLANES, D_MODEL), np.float64)
    for g, w in enumerate(POOL_WINDOWS):
        for r in range(SUBLANES):
            for tab, t in ((first, r), (last, seq - SUBLANES + r)):
                lo = min(max(t - w // 2, 0), seq)
                hi = min(max(t - w // 2 + w, 0), seq)
                tab[r, g * POOL_GROUP_DIM:(g + 1) * POOL_GROUP_DIM] = 1.0 / (hi - lo)
    interior = np.repeat(1.0 / np.array(POOL_WINDOWS, np.float64), POOL_GROUP_DIM)[None, :]
    return first.astype(np.float32), last.astype(np.float32), interior.astype(np.float32)


def _mod_kernel(cond_ref, w_ref, b_ref, o_ref):
    a = _silu(cond_ref[...]).astype(BF16)
    o_ref[...] = jnp.dot(a, w_ref[...].astype(BF16), preferred_element_type=F32) + b_ref[...]


def _modulation(cond, w_mod, b_mod):
    d, n = w_mod.shape
    tn = 1536
    return pl.pallas_call(
        _mod_kernel,
        grid=(n // tn,),
        in_specs=[
            pl.BlockSpec((COND_ROWS, d), lambda j: (0, 0)),
            pl.BlockSpec((d, tn), lambda j: (0, j)),
            pl.BlockSpec((1, tn), lambda j: (0, j)),
        ],
        out_specs=pl.BlockSpec((COND_ROWS, tn), lambda j: (0, j)),
        out_shape=jax.ShapeDtypeStruct((COND_ROWS, n), F32),
        compiler_params=_compiler_params(("arbitrary",)),
        name="modulation",
    )(cond, w_mod, b_mod.reshape(1, n))


def _norm_mod(x, g, scale, shift):
    ms = jnp.mean(x * x, axis=-1, keepdims=True)
    return x * lax.rsqrt(ms + RMS_EPS) * g * (1.0 + scale) + shift


IN0_CHUNK = 512


def _rope(x, cos, sin, partner_is_right):
    rot = jnp.where(partner_is_right, pltpu.roll(x, HEAD_DIM - ROPE_PAIRS, 1), pltpu.roll(x, ROPE_PAIRS, 1))
    return x * cos + rot * sin


def _inproj0_kernel(n_side, x_ref, g_ref, scale_ref, shift_ref, w_ref, cos_ref, sin_ref, *rest):
    side_in, (q_ref, k_ref, v_ref, f_ref, sz_ref) = rest[:n_side], rest[n_side:n_side + 5]
    side_out, h_scr = rest[n_side + 5:2 * n_side + 5], rest[2 * n_side + 5]
    tm = x_ref.shape[1]
    q_scale = HEAD_DIM ** -0.5 * math.log2(math.e)
    lane = lax.broadcasted_iota(jnp.int32, (tm // ROW_SPLIT, HEAD_DIM), 1)
    partner_is_right = (lane % (2 * ROPE_PAIRS)) < ROPE_PAIRS
    heads_per_chunk = IN0_CHUNK // HEAD_DIM
    row_sets = [slice(r0, r0 + tm // ROW_SPLIT) for r0 in range(0, tm, tm // ROW_SPLIT)]

    def normalize(rows):
        h_scr[rows, :] = _norm_mod(x_ref[0, rows, :], g_ref[...], scale_ref[0], shift_ref[0]).astype(BF16)

    normalize(row_sets[0])
    for c0 in range(0, IN0_WIDTH, IN0_CHUNK):
        for si, rows in enumerate(row_sets):
            acc = jnp.dot(h_scr[rows, :], w_ref[:, c0:c0 + IN0_CHUNK], preferred_element_type=F32)
            if c0 == 0 and si + 1 < len(row_sets):
                normalize(row_sets[si + 1])
            if c0 == IN0_CHUNK and si == 0:
                for src, dst in zip(side_in, side_out):
                    dst[...] = src[...].astype(dst.dtype)
            if c0 < ATTN_WIDTH + KV_WIDTH:
                is_q = c0 < ATTN_WIDTH
                dst, off = (q_ref, c0) if is_q else (k_ref, c0 - ATTN_WIDTH)
                cs, sn = cos_ref[rows, :], sin_ref[rows, :]
                if is_q:
                    cs, sn = cs * q_scale, sn * q_scale
                for j in range(heads_per_chunk):
                    xh = acc[:, j * HEAD_DIM:(j + 1) * HEAD_DIM]
                    lo = off + j * HEAD_DIM
                    dst[0, rows, lo:lo + HEAD_DIM] = _rope(xh, cs, sn, partner_is_right).astype(dst.dtype)
            elif c0 < ATTN_WIDTH + 2 * KV_WIDTH:
                v_ref[0, rows, :] = acc.astype(v_ref.dtype)
            elif c0 < ATTN_WIDTH + 2 * KV_WIDTH + FOURIER_WIDTH:
                f_ref[0, rows, :] = acc
            else:
                off = c0 - (IN0_WIDTH - D_MODEL)
                sz_ref[0, rows, off:off + IN0_CHUNK] = _silu(acc).astype(sz_ref.dtype)


def _inproj0(x, norm_g, scale, shift, w_in, cos, sin, side_weights, tm=512):
    b, s, d = x.shape
    steps = s // tm
    vec = pl.BlockSpec((1, 1, d), lambda bi, i: (bi, 0, 0))
    tok = lambda width: pl.BlockSpec((1, tm, width), lambda bi, i: (bi, i, 0))
    tab = pl.BlockSpec((tm, HEAD_DIM), lambda bi, i: (i, 0))
    slab = lambda w: pl.BlockSpec((w.shape[0] // (b * steps), w.shape[1]), lambda bi, i: (bi * steps + i, 0))
    for w in side_weights:
        assert w.shape[0] % (b * steps * BF16_SUBLANES) == 0, w.shape
    outs = pl.pallas_call(
        functools.partial(_inproj0_kernel, len(side_weights)),
        grid=(b, steps),
        in_specs=[tok(d), _resident((1, d), lambda bi, i: (0, 0)), vec, vec,
                  _resident((d, IN0_WIDTH), lambda bi, i: (0, 0)), tab, tab] + [slab(w) for w in side_weights],
        out_specs=[tok(ATTN_WIDTH), tok(KV_WIDTH), tok(KV_WIDTH), tok(FOURIER_WIDTH), tok(D_MODEL)]
                  + [slab(w) for w in side_weights],
        out_shape=[
            jax.ShapeDtypeStruct((b, s, ATTN_WIDTH), BF16),
            jax.ShapeDtypeStruct((b, s, KV_WIDTH), BF16),
            jax.ShapeDtypeStruct((b, s, KV_WIDTH), BF16),
            jax.ShapeDtypeStruct((b, s, FOURIER_WIDTH), F32),
            jax.ShapeDtypeStruct((b, s, D_MODEL), BF16),
        ] + [jax.ShapeDtypeStruct(w.shape, BF16) for w in side_weights],
        scratch_shapes=[pltpu.VMEM((tm, d), BF16)],
        compiler_params=_compiler_params(("parallel", "parallel")),
        name="inproj0",
    )(x, norm_g.reshape(1, d), scale, shift, w_in, cos, sin, *side_weights)
    return outs[:5], outs[5:]


def _ctx_kv_kernel(c_ref, g_ref, scale_ref, shift_ref, wk_ref, wv_ref, k_ref, v_ref):
    h = _norm_mod(c_ref[0], g_ref[...], scale_ref[...], shift_ref[...]).astype(BF16)
    k_ref[0] = jnp.dot(h, wk_ref[...], preferred_element_type=F32).astype(k_ref.dtype)
    v_ref[0] = jnp.dot(h, wv_ref[...], preferred_element_type=F32).astype(v_ref.dtype)


def _ctx_kv(ctx, norm_g, scale_c, shift_c, w_in):
    b, c, d = ctx.shape
    vec = pl.BlockSpec((1, d), lambda bi: (0, 0))
    kblk = ATTN_WIDTH // KV_WIDTH
    out = pl.BlockSpec((1, c, KV_WIDTH), lambda bi: (bi, 0, 0))
    return pl.pallas_call(
        _ctx_kv_kernel,
        grid=(b,),
        in_specs=[pl.BlockSpec((1, c, d), lambda bi: (bi, 0, 0)), vec, vec, vec,
                  pl.BlockSpec((d, KV_WIDTH), lambda bi: (0, kblk)),
                  pl.BlockSpec((d, KV_WIDTH), lambda bi: (0, kblk + 1))],
        out_specs=[out, out],
        out_shape=[jax.ShapeDtypeStruct((b, c, KV_WIDTH), BF16)] * 2,
        compiler_params=_compiler_params(("arbitrary",)),
        name="ctx_kv",
    )(ctx, norm_g.reshape(1, d), scale_c, shift_c, w_in, w_in)


ATTN_TQ = 2048
Q_ROWS = GQA_GROUP * WINDOW


def _head_cols(h):
    return slice(h * HEAD_DIM, (h + 1) * HEAD_DIM)


def _attn_kernel(sink_ref, q_ref, k_ref, kp_ref, kn_ref, v_ref, vp_ref, vn_ref, kc_ref, vc_ref,
                 o_ref, kext, vaug, vcaug, s0, s1, p0, p1, d0, d1):
    i = pl.program_id(1)
    nb = ATTN_TQ // WINDOW
    nblk_total = pl.num_programs(1) * nb
    n_ctx = kc_ref.shape[1]
    s_scr, p_scr, d_scr = (s0, s1), (p0, p1), (d0, d1)

    kext[0:WINDOW] = kp_ref[0]
    kext[WINDOW:WINDOW + ATTN_TQ] = k_ref[0]
    kext[WINDOW + ATTN_TQ:] = kn_ref[0]
    ones = jnp.ones((ATTN_TQ + 2 * WINDOW, HEAD_DIM), BF16)
    for kh in range(N_KV_HEADS):
        vaug[kh, 0:WINDOW, 0:HEAD_DIM] = vp_ref[0, :, _head_cols(kh)]
        vaug[kh, WINDOW:WINDOW + ATTN_TQ, 0:HEAD_DIM] = v_ref[0, :, _head_cols(kh)]
        vaug[kh, WINDOW + ATTN_TQ:, 0:HEAD_DIM] = vn_ref[0, :, _head_cols(kh)]
        vaug[kh, :, HEAD_DIM:] = ones
        vcaug[kh, :, 0:HEAD_DIM] = vc_ref[0, :, _head_cols(kh)]
        vcaug[kh, :, HEAD_DIM:] = ones[0:n_ctx]

    r = lax.broadcasted_iota(jnp.int32, (Q_ROWS, WINDOW), 0) % WINDOW
    c = lax.broadcasted_iota(jnp.int32, (Q_ROWS, WINDOW), 1)
    neg = jnp.float32(-jnp.inf)
    bias_prev = jnp.where(c >= r, 0.0, neg).astype(F32)
    bias_next = jnp.where(c <= r, 0.0, neg).astype(F32)
    nt = (((1,), (1,)), ((), ()))

    def scores(j, kh, s_ref):
        base = pl.multiple_of(j * WINDOW, WINDOW)
        n = i * nb + j
        q3 = jnp.concatenate([q_ref[0, pl.ds(base, WINDOW), _head_cols(kh * GQA_GROUP + g)]
                              for g in range(GQA_GROUP)], axis=0)
        s_ctx = lax.dot_general(q3, kc_ref[0, :, _head_cols(kh)], nt, preferred_element_type=F32)
        s_win = lax.dot_general(q3, kext[pl.ds(base, 3 * WINDOW), _head_cols(kh)], nt,
                                preferred_element_type=F32)
        s_ref[:, 0:n_ctx] = s_ctx
        s_ref[:, n_ctx:n_ctx + WINDOW] = s_win[:, 0:WINDOW] + jnp.where(n > 0, bias_prev, neg)
        s_ref[:, n_ctx + WINDOW:n_ctx + 2 * WINDOW] = s_win[:, WINDOW:2 * WINDOW]
        s_ref[:, n_ctx + 2 * WINDOW:] = s_win[:, 2 * WINDOW:] + jnp.where(n < nblk_total - 1, bias_next, neg)

    def softmax(kh, s_ref, p_ref, d_ref):
        ntile = s_ref.shape[1] // LANES
        tiles = [s_ref[:, t * LANES:(t + 1) * LANES] for t in range(ntile)]
        mx = tiles[0]
        for t in range(1, ntile):
            mx = jnp.maximum(mx, tiles[t])
        sink = jnp.concatenate([jnp.full((WINDOW, 1), sink_ref[kh * GQA_GROUP + g] * math.log2(math.e), F32)
                                for g in range(GQA_GROUP)], axis=0)
        m = jnp.maximum(jnp.max(mx, axis=1, keepdims=True), sink)
        for t in range(ntile):
            p_ref[:, t * LANES:(t + 1) * LANES] = jnp.exp2(tiles[t] - m).astype(BF16)
        d_ref[...] = jnp.broadcast_to(jnp.exp2(sink - m), (Q_ROWS, HEAD_DIM))

    def values(j, kh, p_ref, d_ref):
        base = pl.multiple_of(j * WINDOW, WINDOW)
        o = jnp.dot(p_ref[:, 0:n_ctx], vcaug[kh], preferred_element_type=F32)
        o = o + jnp.dot(p_ref[:, n_ctx:], vaug[kh, pl.ds(base, 3 * WINDOW), :], preferred_element_type=F32)
        res = o[:, 0:HEAD_DIM] * (1.0 / (o[:, HEAD_DIM:] + d_ref[...]))
        for g in range(GQA_GROUP):
            o_ref[0, pl.ds(base, WINDOW), _head_cols(kh * GQA_GROUP + g)] = (
                res[g * WINDOW:(g + 1) * WINDOW].astype(o_ref.dtype))

    def tick(j, kh, do_scores=True, do_softmax=True, do_values=True):
        par = kh % 2
        if do_scores:
            scores(j, kh, s_scr[par])
        if do_softmax:
            khb = (kh - 1) % N_KV_HEADS
            softmax(khb, s_scr[1 - par], p_scr[1 - par], d_scr[1 - par])
        if do_values:
            jc, khc = (j, kh - 2) if kh >= 2 else (j - 1, kh + 2)
            values(jc, khc, p_scr[par], d_scr[par])

    tick(0, 0, do_softmax=False, do_values=False)
    tick(0, 1, do_values=False)
    tick(0, 2)
    tick(0, 3)

    def block(j, carry):
        for kh in range(N_KV_HEADS):
            tick(j, kh)
        return carry

    lax.fori_loop(1, nb, block, 0)
    tick(nb, 0, do_scores=False)
    tick(nb, 1, do_scores=False, do_softmax=False)


def _attention(q, k, v, kc, vc, sink):
    b, s, _ = q.shape
    c = kc.shape[1]
    steps = s // ATTN_TQ
    per = ATTN_TQ // WINDOW
    nblk = s // WINDOW
    main = lambda width: pl.BlockSpec((1, ATTN_TQ, width), lambda bi, i: (bi, i, 0))
    prev = pl.BlockSpec((1, WINDOW, KV_WIDTH), lambda bi, i: (bi, jnp.maximum(i * per - 1, 0), 0))
    nxt = pl.BlockSpec((1, WINDOW, KV_WIDTH), lambda bi, i: (bi, jnp.minimum((i + 1) * per, nblk - 1), 0))
    ctx = pl.BlockSpec((1, c, KV_WIDTH), lambda bi, i: (bi, 0, 0))
    return pl.pallas_call(
        _attn_kernel,
        grid=(b, steps),
        in_specs=[pl.BlockSpec(memory_space=pltpu.SMEM), main(ATTN_WIDTH),
                  main(KV_WIDTH), prev, nxt, main(KV_WIDTH), prev, nxt, ctx, ctx],
        out_specs=main(ATTN_WIDTH),
        out_shape=jax.ShapeDtypeStruct((b, s, ATTN_WIDTH), BF16),
        scratch_shapes=[pltpu.VMEM((ATTN_TQ + 2 * WINDOW, KV_WIDTH), BF16),
                        pltpu.VMEM((N_KV_HEADS, ATTN_TQ + 2 * WINDOW, 2 * HEAD_DIM), BF16),
                        pltpu.VMEM((N_KV_HEADS, c, 2 * HEAD_DIM), BF16)]
                       + [pltpu.VMEM((Q_ROWS, c + 3 * WINDOW), F32)] * 2
                       + [pltpu.VMEM((Q_ROWS, c + 3 * WINDOW), BF16)] * 2
                       + [pltpu.VMEM((Q_ROWS, HEAD_DIM), F32)] * 2,
        compiler_params=_compiler_params(("parallel", "parallel")),
        name="window_attn",
    )(sink, q, k, k, k, v, v, v, kc, vc)


FFT_SUB = SUBLANES
FFT_BLOCKS = 4
FFT_STEP = FFT_BLOCKS * FFT_SUB
FFT_STEPS = FFT_RADIX // FFT_STEP


def _fourier_kernel(*refs):
    nblk = FFT_BLOCKS
    x_refs = refs[:nblk]
    s1_ref, s2_ref, chan_ref, wf_ref, tw_ref, o_ref, hr_scr, hi_scr, w_scr = refs[nblk:nblk + 9]
    scr = refs[nblk + 9:]
    yr_scr, yi_scr, z_scr = scr[0:nblk], scr[nblk:2 * nblk], scr[2 * nblk:3 * nblk]
    r = FFT_RADIX
    t = pl.program_id(2)

    @pl.when(t == 0)
    def _():
        w = jnp.dot(chan_ref[...], wf_ref[0].astype(BF16), preferred_element_type=F32)
        w_scr[:, 0:r] = w[0:r].astype(BF16)
        w_scr[:, r:2 * r] = w[r:2 * r].astype(BF16)


    def channel_stage(i):
        x2 = x_refs[i][...].reshape(r * FFT_SUB, LANES).astype(BF16)
        y = jnp.dot(x2, w_scr[...], preferred_element_type=F32)
        yr_scr[i][...] = y[:, 0:r]
        yi_scr[i][...] = y[:, r:2 * r]

    def n2_stage(i):
        top = jnp.concatenate([yr_scr[i][pl.ds(j, r, stride=FFT_SUB), :].astype(BF16) for j in range(FFT_SUB)], axis=1)
        bot = jnp.concatenate([yi_scr[i][pl.ds(j, r, stride=FFT_SUB), :].astype(BF16) for j in range(FFT_SUB)], axis=1)
        h = jnp.dot(s1_ref[...], jnp.concatenate([top, bot], axis=0), preferred_element_type=F32)
        for j in range(FFT_SUB):
            n1 = (nblk * t + i) * FFT_SUB + j
            row = pl.multiple_of(n1 * FFT_PITCH, SUBLANES)
            hr_scr[pl.ds(row, r), :] = h[0:r, j * LANES:(j + 1) * LANES]
            hi_scr[pl.ds(row, r), :] = h[r:2 * r, j * LANES:(j + 1) * LANES]

    @pl.when(t < FFT_STEPS)
    def _():
        for i in range(nblk + 1):
            if i < nblk:
                channel_stage(i)
            if i > 0:
                n2_stage(i - 1)

    def twiddle_stage(i):
        tops, bots = [], []
        for j in range(i * FFT_SUB, (i + 1) * FFT_SUB):
            v = (t - FFT_STEPS) * FFT_STEP + j
            hr = hr_scr[pl.ds(v, r, stride=FFT_PITCH), :]
            hi = hi_scr[pl.ds(v, r, stride=FFT_PITCH), :]
            tc = tw_ref[0, v].astype(F32)
            ts = tw_ref[1, v].astype(F32)
            tops.append((hr * tc + hi * ts).astype(BF16))
            bots.append((hi * tc - hr * ts).astype(BF16))
        return jnp.concatenate([jnp.concatenate(tops, axis=1), jnp.concatenate(bots, axis=1)], axis=0)

    def n1_stage(i, rhs):
        z = jnp.dot(s2_ref[...], rhs, preferred_element_type=F32)
        for j in range(FFT_SUB):
            z_scr[i][pl.ds(j, r, stride=FFT_SUB), :] = z[:, j * LANES:(j + 1) * LANES]
        o_ref[:, i * FFT_SUB:(i + 1) * FFT_SUB, :] = z_scr[i][...].reshape(r, FFT_SUB, LANES)

    @pl.when(t >= FFT_STEPS)
    def _():
        rhs = [None] * nblk
        for i in range(nblk + 1):
            if i < nblk:
                rhs[i] = twiddle_stage(i)
            if i > 0:
                n1_stage(i - 1, rhs[i - 1])


def _fourier_mix(f, w_f):
    b, s, _ = f.shape
    r = FFT_RADIX
    assert s == r * r
    s1, s2, chan, tw = _dft_tables()
    f5 = f.reshape(b, r, r // FFT_SUB, FFT_SUB, FOURIER_WIDTH)
    last = r // FFT_SUB - 1
    x_spec = lambda i: pl.BlockSpec(
        (None, r, None, FFT_SUB, LANES), lambda bi, g, t: (bi, 0, jnp.minimum(FFT_BLOCKS * t + i, last), 0, g))
    o_spec = pl.BlockSpec((None, r, None, FFT_STEP, LANES),
                          lambda bi, g, t: (bi, 0, jnp.maximum(t - FFT_STEPS, 0), 0, g))
    const = lambda shape: _resident(shape, lambda bi, g, t: (0,) * len(shape))
    z5 = pl.pallas_call(
        _fourier_kernel,
        grid=(b, FOURIER_GROUPS, 2 * FFT_STEPS),
        in_specs=[x_spec(i) for i in range(FFT_BLOCKS)]
                 + [const((2 * r, 2 * r)), const((r, 2 * r)), const((2 * r, r)),
                    pl.BlockSpec((1, r, r), lambda bi, g, t: (g, 0, 0)), const((2, r, r, LANES))],
        out_specs=o_spec,
        out_shape=jax.ShapeDtypeStruct((b, r, FFT_STEPS, FFT_STEP, FOURIER_WIDTH), F32),
        scratch_shapes=[pltpu.VMEM((r * FFT_PITCH, LANES), F32), pltpu.VMEM((r * FFT_PITCH, LANES), F32),
                        pltpu.VMEM((r, 2 * r), BF16)]
                       + [pltpu.VMEM((r * FFT_SUB, LANES), F32)] * (3 * FFT_BLOCKS),
        compiler_params=_compiler_params(("parallel", "parallel", "arbitrary")),
        name="fourier_mix",
    )(*([f5] * FFT_BLOCKS), *(jnp.asarray(a).astype(BF16) for a in (s1, s2, chan)), w_f,
      jnp.asarray(tw).astype(BF16))
    return z5.reshape(b, s, FOURIER_WIDTH)


def _outproj0_kernel(a_ref, zf_ref, sz_ref, x_ref, gate_ref, w_ref, o_ref):
    ya = a_ref[0] * sz_ref[0, :, 0:ATTN_WIDTH]
    yf = zf_ref[0].astype(BF16) * sz_ref[0, :, ATTN_WIDTH:]
    acc = jnp.dot(ya, w_ref[0:ATTN_WIDTH, :], preferred_element_type=F32)
    acc = acc + jnp.dot(yf, w_ref[ATTN_WIDTH:, :], preferred_element_type=F32)
    o_ref[0] = x_ref[0] + gate_ref[0] * acc


def _outproj0(attn, zf, sz, x, gate, w_out, tm=512):
    b, s, d = x.shape
    tok = lambda width: pl.BlockSpec((1, tm, width), lambda bi, i: (bi, i, 0))
    return pl.pallas_call(
        _outproj0_kernel,
        grid=(b, s // tm),
        in_specs=[tok(ATTN_WIDTH), tok(FOURIER_WIDTH), tok(d), tok(d),
                  pl.BlockSpec((1, 1, d), lambda bi, i: (bi, 0, 0)),
                  _resident((d, d), lambda bi, i: (0, 0))],
        out_specs=tok(d),
        out_shape=jax.ShapeDtypeStruct((b, s, d), F32),
        compiler_params=_compiler_params(("parallel", "parallel")),
        name="outproj0",
    )(attn, zf, sz, x, gate, w_out)


IN1_CHUNK = 512


def _inproj1_kernel(x_ref, g_ref, scale_ref, shift_ref, w_ref, u_ref, sz_ref, h_scr):
    tm = x_ref.shape[1]
    row_sets = [slice(r0, r0 + tm // ROW_SPLIT) for r0 in range(0, tm, tm // ROW_SPLIT)]

    def normalize(rows):
        h_scr[rows, :] = _norm_mod(x_ref[0, rows, :], g_ref[...], scale_ref[0], shift_ref[0]).astype(BF16)

    normalize(row_sets[0])
    for c0 in range(0, 2 * D_MODEL, IN1_CHUNK):
        for si, rows in enumerate(row_sets):
            acc = jnp.dot(h_scr[rows, :], w_ref[:, c0:c0 + IN1_CHUNK], preferred_element_type=F32)
            if c0 == 0 and si + 1 < len(row_sets):
                normalize(row_sets[si + 1])
            if c0 < D_MODEL:
                u_ref[0, rows, c0:c0 + IN1_CHUNK] = acc.astype(u_ref.dtype)
            else:
                sz_ref[0, rows, c0 - D_MODEL:c0 - D_MODEL + IN1_CHUNK] = _silu(acc).astype(sz_ref.dtype)


def _inproj1(x, norm_g, scale, shift, w_in, tm=1024):
    b, s, d = x.shape
    vec = pl.BlockSpec((1, 1, d), lambda bi, i: (bi, 0, 0))
    tok = pl.BlockSpec((1, tm, d), lambda bi, i: (bi, i, 0))
    return pl.pallas_call(
        _inproj1_kernel,
        grid=(b, s // tm),
        in_specs=[tok, _resident((1, d), lambda bi, i: (0, 0)), vec, vec,
                  _resident((d, 2 * d), lambda bi, i: (0, 0))],
        out_specs=[tok, tok],
        out_shape=[jax.ShapeDtypeStruct((b, s, d), BF16)] * 2,
        scratch_shapes=[pltpu.VMEM((tm, d), BF16)],
        compiler_params=_compiler_params(("parallel", "parallel")),
        name="inproj1",
    )(x, norm_g.reshape(1, d), scale, shift, w_in)


POOL_HALO = BF16_SUBLANES
POOL_SUB = 128
POOL_PAD = 64


@functools.lru_cache(maxsize=None)
def _pool_band():
    p = np.arange(POOL_SUB)[:, None]
    e = np.arange(POOL_SUB + 2 * POOL_PAD)[None, :] - POOL_PAD
    return np.stack([((e >= p - w // 2) & (e <= p - w // 2 + w - 1)) for w in POOL_WINDOWS]).astype(np.float32)


def _pool_out_kernel(u_ref, up_ref, un_ref, sz_ref, x_ref, gate_ref, ls_ref, fn_ref,
                     inv_first_ref, inv_last_ref, inv_mid_ref, band_ref, wg_ref, wo_ref, o_ref, ext_scr, y_scr):
    i = pl.program_id(1)
    last = pl.num_programs(1) - 1
    tm = u_ref.shape[1]
    pad = POOL_PAD
    zeros = jnp.zeros((pad - POOL_HALO, u_ref.shape[2]), BF16)
    ext_scr[0:pad - POOL_HALO] = zeros
    ext_scr[pad - POOL_HALO:pad] = jnp.where(i > 0, up_ref[0], jnp.zeros_like(up_ref[0]))
    ext_scr[pad:pad + tm] = u_ref[0]
    ext_scr[pad + tm:pad + tm + POOL_HALO] = jnp.where(i < last, un_ref[0], jnp.zeros_like(un_ref[0]))
    ext_scr[pad + tm + POOL_HALO:] = zeros
    for g, w in enumerate(POOL_WINDOWS):
        cols = slice(g * POOL_GROUP_DIM, (g + 1) * POOL_GROUP_DIM)
        uc = u_ref[0, :, cols].astype(F32)
        wsum = jnp.concatenate(
            [jnp.dot(band_ref[g], ext_scr[r0:r0 + POOL_SUB + 2 * pad, cols], preferred_element_type=F32)
             for r0 in range(0, tm, POOL_SUB)], axis=0)
        inv_mid = inv_mid_ref[:, cols]
        inv_head = jnp.where(i == 0, inv_first_ref[:, cols], inv_mid)
        inv_tail = jnp.where(i == last, inv_last_ref[:, cols], inv_mid)
        mean = jnp.concatenate([wsum[0:SUBLANES] * inv_head,
                                wsum[SUBLANES:tm - SUBLANES] * inv_mid,
                                wsum[tm - SUBLANES:] * inv_tail], axis=0)
        pooled = (mean - uc).astype(BF16)
        y_scr[:, cols] = jnp.dot(pooled, wg_ref[g], preferred_element_type=F32)

    yz = (y_scr[...] * ls_ref[...]).astype(BF16) * sz_ref[0]
    acc = jnp.dot(yz, wo_ref[...], preferred_element_type=F32)
    x2 = x_ref[0] + gate_ref[0] * acc
    ms = jnp.mean(x2 * x2, axis=-1, keepdims=True)
    o_ref[0] = x2 * lax.rsqrt(ms + RMS_EPS) * fn_ref[...]


def _pool_out(u, sz, x, gate, layer_scale, final_norm, w_grp, w_out, tm=512):
    b, s, d = x.shape
    per = tm // POOL_HALO
    nh = s // POOL_HALO
    inv_first, inv_last, inv_mid = _pool_edge_tables(s)
    tok = pl.BlockSpec((1, tm, d), lambda bi, i: (bi, i, 0))
    prev = pl.BlockSpec((1, POOL_HALO, d), lambda bi, i: (bi, jnp.maximum(i * per - 1, 0), 0))
    nxt = pl.BlockSpec((1, POOL_HALO, d), lambda bi, i: (bi, jnp.minimum((i + 1) * per, nh - 1), 0))
    row = lambda rows: _resident((rows, d), lambda bi, i: (0, 0))
    return pl.pallas_call(
        _pool_out_kernel,
        grid=(b, s // tm),
        in_specs=[tok, prev, nxt, tok, tok, pl.BlockSpec((1, 1, d), lambda bi, i: (bi, 0, 0)),
                  row(1), row(1), row(SUBLANES), row(SUBLANES), row(1),
                  _resident((POOL_GROUPS, POOL_SUB, POOL_SUB + 2 * POOL_PAD), lambda bi, i: (0, 0, 0)),
                  _resident((POOL_GROUPS, POOL_GROUP_DIM, POOL_GROUP_DIM), lambda bi, i: (0, 0, 0)),
                  _resident((d, d), lambda bi, i: (0, 0))],
        out_specs=tok,
        out_shape=jax.ShapeDtypeStruct((b, s, d), F32),
        scratch_shapes=[pltpu.VMEM((tm + 2 * POOL_PAD, d), BF16), pltpu.VMEM((tm, d), F32)],
        compiler_params=_compiler_params(("parallel", "parallel")),
        name="pool_out",
    )(u, u, u, sz, x, gate, layer_scale.reshape(1, d), final_norm.reshape(1, d),
      jnp.asarray(inv_first), jnp.asarray(inv_last), jnp.asarray(inv_mid), jnp.asarray(_pool_band()).astype(BF16),
      w_grp, w_out)


def kernel(x, c, ctx, c_ctx, l0_norm, l0_w_mod, l0_b_mod, l0_w_in, l0_sink, l0_w_f, l0_w_out,
           l1_norm, l1_w_mod, l1_b_mod, l1_w_in, l1_w_grp, l1_scale, l1_w_out, final_norm):
    b, s, d = x.shape
    cond = jnp.concatenate([c, c_ctx[None, :], jnp.zeros((COND_ROWS - b - 1, d), F32)], axis=0)
    m0 = _modulation(cond, l0_w_mod, l0_b_mod)
    m1 = _modulation(cond, l1_w_mod, l1_b_mod)
    tok_vec = lambda m, k: m[:b, k * d:(k + 1) * d].reshape(b, 1, d)
    shift0, scale0, gate0 = (tok_vec(m0, k) for k in range(3))
    shift1, scale1, gate1 = (tok_vec(m1, k) for k in range(3))
    shift_c, scale_c = m0[b:b + 1, 0:d], m0[b:b + 1, d:2 * d]

    w_in0 = l0_w_in.astype(BF16)
    cos, sin = _rope_tables(s)
    side = [l0_w_out, l1_w_in, l1_w_grp.reshape(POOL_GROUPS * POOL_GROUP_DIM, POOL_GROUP_DIM), l1_w_out]
    (q, k, v, f, sz0), (w_out0, w_in1, w_grp, w_out1) = _inproj0(
        x, l0_norm, scale0, shift0, w_in0, jnp.asarray(cos), jnp.asarray(sin), side)
    kc, vc = _ctx_kv(ctx, l0_norm, scale_c, shift_c, w_in0)
    attn = _attention(q, k, v, kc, vc, l0_sink)
    zf = _fourier_mix(f, l0_w_f)
    x1 = _outproj0(attn, zf, sz0, x, gate0, w_out0)

    u, sz1 = _inproj1(x1, l1_norm, scale1, shift1, w_in1)
    return _pool_out(u, sz1, x1, gate1, l1_scale, final_norm, w_grp.reshape(l1_w_grp.shape), w_out1)
```

```python
import functools
import math

import numpy as np
import jax
import jax.numpy as jnp
from jax import lax
from jax.experimental import pallas as pl
from jax.experimental.pallas import tpu as pltpu

F32 = jnp.float32
BF16 = jnp.bfloat16

D_MODEL = 2048
GRID_W = 64
HEAD_DIM = 128
N_Q_HEADS = 12
N_KV_HEADS = 4
GQA_GROUP = N_Q_HEADS // N_KV_HEADS
WINDOW = 128
ATTN_WIDTH = N_Q_HEADS * HEAD_DIM
KV_WIDTH = N_KV_HEADS * HEAD_DIM
FOURIER_WIDTH = D_MODEL - ATTN_WIDTH
FOURIER_GROUPS = 4
FOURIER_GROUP_DIM = FOURIER_WIDTH // FOURIER_GROUPS
IN0_WIDTH = ATTN_WIDTH + 2 * KV_WIDTH + FOURIER_WIDTH + D_MODEL
POOL_WINDOWS = (2, 4, 8, 16)
POOL_GROUPS = len(POOL_WINDOWS)
POOL_GROUP_DIM = D_MODEL // POOL_GROUPS
ROPE_BASE = 10000.0
ROPE_PAIRS = HEAD_DIM // 4
RMS_EPS = 1e-6

LANES = 128
SUBLANES = 8
BF16_SUBLANES = 16
VMEM_LIMIT_BYTES = 56 * 1024 * 1024

FFT_RADIX = 128
FFT_PITCH = FFT_RADIX + SUBLANES

COND_ROWS = 8
MOD_TN = 1536


ROW_SET = 256


def _silu(x):
    hx = 0.5 * x
    return hx + hx * jnp.tanh(hx)


def _compiler_params(semantics):
    return pltpu.CompilerParams(dimension_semantics=semantics, vmem_limit_bytes=VMEM_LIMIT_BYTES)


def _resident(block_shape, index_map):
    return pl.BlockSpec(block_shape, index_map, pipeline_mode=pl.Buffered(1))


@functools.lru_cache(maxsize=None)
def _rope_tables(seq):
    pos = np.arange(seq)
    row = (pos // GRID_W).astype(np.float64)
    col = (pos % GRID_W).astype(np.float64)
    inv = 1.0 / (ROPE_BASE ** (np.arange(ROPE_PAIRS, dtype=np.float64) / ROPE_PAIRS))
    ar, ac = row[:, None] * inv, col[:, None] * inv
    cos = np.concatenate([np.cos(ar), np.cos(ar), np.cos(ac), np.cos(ac)], axis=1)
    sin = np.concatenate([-np.sin(ar), np.sin(ar), -np.sin(ac), np.sin(ac)], axis=1)
    return cos.astype(np.float32), sin.astype(np.float32)


@functools.lru_cache(maxsize=None)
def _dft_tables():
    r = FFT_RADIX
    idx = np.arange(r)
    ang = 2.0 * np.pi * np.outer(idx, idx) / r
    c, s = np.cos(ang) / math.sqrt(r), np.sin(ang) / math.sqrt(r)
    stage1 = np.block([[c, s], [-s, c]])
    stage2 = np.concatenate([c, s], axis=1)
    chan = np.concatenate([c, -s], axis=0)
    ang2 = 2.0 * np.pi * ((idx[:, None] * idx[None, :]) % (r * r)) / (r * r)
    tw = np.stack([np.cos(ang2), np.sin(ang2)])[..., None] * np.ones(LANES)
    return (stage1.astype(np.float32), stage2.astype(np.float32), chan.astype(np.float32),
            tw.astype(np.float32))


@functools.lru_cache(maxsize=None)
def _pool_edge_tables(seq):
    first = np.zeros((SUBLANES, D_MODEL), np.float64)
    last = np.zeros((SUBLANES, D_MODEL), np.float64)
    for g, w in enumerate(POOL_WINDOWS):
        for r in range(SUBLANES):
            for tab, t in ((first, r), (last, seq - SUBLANES + r)):
                lo = min(max(t - w // 2, 0), seq)
                hi = min(max(t - w // 2 + w, 0), seq)
                tab[r, g * POOL_GROUP_DIM:(g + 1) * POOL_GROUP_DIM] = 1.0 / (hi - lo)
    interior = np.repeat(1.0 / np.array(POOL_WINDOWS, np.float64), POOL_GROUP_DIM)[None, :]
    return first.astype(np.float32), last.astype(np.float32), interior.astype(np.float32)


def _mod_kernel(cond_ref, w_ref, b_ref, o_ref):
    a = _silu(cond_ref[...]).astype(BF16)
    o_ref[...] = jnp.dot(a, w_ref[...].astype(BF16), preferred_element_type=F32) + b_ref[...]


def _modulation(cond, w_mod, b_mod):
    d, n = w_mod.shape
    tn = MOD_TN
    return pl.pallas_call(
        _mod_kernel,
        grid=(n // tn,),
        in_specs=[
            pl.BlockSpec((COND_ROWS, d), lambda j: (0, 0)),
            pl.BlockSpec((d, tn), lambda j: (0, j)),
            pl.BlockSpec((1, tn), lambda j: (0, j)),
        ],
        out_specs=pl.BlockSpec((COND_ROWS, tn), lambda j: (0, j)),
        out_shape=jax.ShapeDtypeStruct((COND_ROWS, n), F32),
        compiler_params=_compiler_params(("arbitrary",)),
        name="modulation",
    )(cond, w_mod, b_mod.reshape(1, n))


MOD_SHIFT, MOD_SCALE, MOD_GATE = 0, 1, 2


def _mod_spec(d, k, row=None):
    if row is None:
        return pl.BlockSpec((1, 1, d), lambda bi, *_: (bi, 0, k))
    return pl.BlockSpec((1, 1, d), lambda bi, *_: (row, 0, k))


def _norm_mod(x, g, scale, shift):
    ms = jnp.mean(x * x, axis=-1, keepdims=True)
    return x * lax.rsqrt(ms + RMS_EPS) * g * (1.0 + scale) + shift


IN0_CHUNK = 512


def _rope(x, cos, sin, partner_is_right):
    rot = jnp.where(partner_is_right, pltpu.roll(x, HEAD_DIM - ROPE_PAIRS, 1), pltpu.roll(x, ROPE_PAIRS, 1))
    return x * cos + rot * sin


def _inproj0_kernel(n_side, x_ref, g_ref, scale_ref, shift_ref, w_ref, cos_ref, sin_ref, *rest):
    side_in, (q_ref, k_ref, v_ref, f_ref, sz_ref) = rest[:n_side], rest[n_side:n_side + 5]
    side_out, h_scr = rest[n_side + 5:2 * n_side + 5], rest[2 * n_side + 5]
    tm = x_ref.shape[1]
    q_scale = HEAD_DIM ** -0.5 * math.log2(math.e)
    lane = lax.broadcasted_iota(jnp.int32, (ROW_SET, HEAD_DIM), 1)
    partner_is_right = (lane % (2 * ROPE_PAIRS)) < ROPE_PAIRS
    heads_per_chunk = IN0_CHUNK // HEAD_DIM
    row_sets = [slice(r0, r0 + ROW_SET) for r0 in range(0, tm, ROW_SET)]

    def normalize(rows):
        h_scr[rows, :] = _norm_mod(x_ref[0, rows, :], g_ref[...], scale_ref[0], shift_ref[0]).astype(BF16)

    normalize(row_sets[0])
    gate_start = IN0_WIDTH - D_MODEL
    order = list(range(gate_start, IN0_WIDTH, IN0_CHUNK)) + list(range(0, gate_start, IN0_CHUNK))
    for ci, c0 in enumerate(order):
        for si, rows in enumerate(row_sets):
            acc = jnp.dot(h_scr[rows, :], w_ref[:, c0:c0 + IN0_CHUNK], preferred_element_type=F32)
            if ci == 0 and si + 1 < len(row_sets):
                normalize(row_sets[si + 1])
            if ci == 1 and si == 0:
                for src, dst in zip(side_in, side_out):
                    dst[...] = src[...].astype(dst.dtype)
            if c0 < ATTN_WIDTH + KV_WIDTH:
                is_q = c0 < ATTN_WIDTH
                dst, off = (q_ref, c0) if is_q else (k_ref, c0 - ATTN_WIDTH)
                cs, sn = cos_ref[rows, :], sin_ref[rows, :]
                if is_q:
                    cs, sn = cs * q_scale, sn * q_scale
                for j in range(heads_per_chunk):
                    xh = acc[:, j * HEAD_DIM:(j + 1) * HEAD_DIM]
                    lo = off + j * HEAD_DIM
                    dst[0, rows, lo:lo + HEAD_DIM] = _rope(xh, cs, sn, partner_is_right).astype(dst.dtype)
            elif c0 < ATTN_WIDTH + 2 * KV_WIDTH:
                v_ref[0, rows, :] = acc.astype(v_ref.dtype)
            elif c0 < ATTN_WIDTH + 2 * KV_WIDTH + FOURIER_WIDTH:
                f_ref[0, rows, :] = acc
            else:
                off = c0 - (IN0_WIDTH - D_MODEL)
                sz_ref[0, rows, off:off + IN0_CHUNK] = _silu(acc).astype(sz_ref.dtype)


def _inproj0(x, norm_g, mod, w_in, cos, sin, side_weights, tm=512):
    b, s, d = x.shape
    steps = s // tm
    tok = lambda width: pl.BlockSpec((1, tm, width), lambda bi, i: (bi, i, 0))
    tab = pl.BlockSpec((tm, HEAD_DIM), lambda bi, i: (i, 0))
    slab = lambda w: pl.BlockSpec((w.shape[0] // (b * steps), w.shape[1]), lambda bi, i: (bi * steps + i, 0))
    for w in side_weights:
        assert w.shape[0] % (b * steps * BF16_SUBLANES) == 0, w.shape
    outs = pl.pallas_call(
        functools.partial(_inproj0_kernel, len(side_weights)),
        grid=(b, steps),
        in_specs=[tok(d), _resident((1, d), lambda bi, i: (0, 0)), _mod_spec(d, MOD_SCALE), _mod_spec(d, MOD_SHIFT),
                  _resident((d, IN0_WIDTH), lambda bi, i: (0, 0)), tab, tab] + [slab(w) for w in side_weights],
        out_specs=[tok(ATTN_WIDTH), tok(KV_WIDTH), tok(KV_WIDTH), tok(FOURIER_WIDTH), tok(D_MODEL)]
                  + [slab(w) for w in side_weights],
        out_shape=[
            jax.ShapeDtypeStruct((b, s, ATTN_WIDTH), BF16),
            jax.ShapeDtypeStruct((b, s, KV_WIDTH), BF16),
            jax.ShapeDtypeStruct((b, s, KV_WIDTH), BF16),
            jax.ShapeDtypeStruct((b, s, FOURIER_WIDTH), F32),
            jax.ShapeDtypeStruct((b, s, D_MODEL), BF16),
        ] + [jax.ShapeDtypeStruct(w.shape, BF16) for w in side_weights],
        scratch_shapes=[pltpu.VMEM((tm, d), BF16)],
        compiler_params=_compiler_params(("parallel", "parallel")),
        name="inproj0",
    )(x, norm_g.reshape(1, d), mod, mod, w_in, cos, sin, *side_weights)
    return outs[:5], outs[5:]


def _ctx_kv_kernel(c_ref, g_ref, scale_ref, shift_ref, wk_ref, wv_ref, k_ref, v_ref):
    h = _norm_mod(c_ref[0], g_ref[...], scale_ref[0], shift_ref[0]).astype(BF16)
    k_ref[0] = jnp.dot(h, wk_ref[...], preferred_element_type=F32).astype(k_ref.dtype)
    v_ref[0] = jnp.dot(h, wv_ref[...], preferred_element_type=F32).astype(v_ref.dtype)


def _ctx_kv(ctx, norm_g, mod, mod_row, w_in):
    b, c, d = ctx.shape
    kblk = ATTN_WIDTH // KV_WIDTH
    out = pl.BlockSpec((1, c, KV_WIDTH), lambda bi: (bi, 0, 0))
    return pl.pallas_call(
        _ctx_kv_kernel,
        grid=(b,),
        in_specs=[pl.BlockSpec((1, c, d), lambda bi: (bi, 0, 0)), pl.BlockSpec((1, d), lambda bi: (0, 0)),
                  _mod_spec(d, MOD_SCALE, mod_row), _mod_spec(d, MOD_SHIFT, mod_row),
                  pl.BlockSpec((d, KV_WIDTH), lambda bi: (0, kblk)),
                  pl.BlockSpec((d, KV_WIDTH), lambda bi: (0, kblk + 1))],
        out_specs=[out, out],
        out_shape=[jax.ShapeDtypeStruct((b, c, KV_WIDTH), BF16)] * 2,
        compiler_params=_compiler_params(("arbitrary",)),
        name="ctx_kv",
    )(ctx, norm_g.reshape(1, d), mod, mod, w_in, w_in)


ATTN_TQ = 2048
Q_ROWS = GQA_GROUP * WINDOW


def _head_cols(h):
    return slice(h * HEAD_DIM, (h + 1) * HEAD_DIM)


def _attn_kernel(sink_ref, q_ref, k_ref, kp_ref, kn_ref, v_ref, vp_ref, vn_ref, kc_ref, vc_ref,
                 o_ref, kext, vaug, vcaug, s0, s1, p0, p1, d0, d1):
    i = pl.program_id(1)
    nb = ATTN_TQ // WINDOW
    nblk_total = pl.num_programs(1) * nb
    n_ctx = kc_ref.shape[1]
    s_scr, p_scr, d_scr = (s0, s1), (p0, p1), (d0, d1)

    kext[0:WINDOW] = kp_ref[0]
    kext[WINDOW:WINDOW + ATTN_TQ] = k_ref[0]
    kext[WINDOW + ATTN_TQ:] = kn_ref[0]
    ones = jnp.ones((ATTN_TQ + 2 * WINDOW, HEAD_DIM), BF16)
    for kh in range(N_KV_HEADS):
        vaug[kh, 0:WINDOW, 0:HEAD_DIM] = vp_ref[0, :, _head_cols(kh)]
        vaug[kh, WINDOW:WINDOW + ATTN_TQ, 0:HEAD_DIM] = v_ref[0, :, _head_cols(kh)]
        vaug[kh, WINDOW + ATTN_TQ:, 0:HEAD_DIM] = vn_ref[0, :, _head_cols(kh)]
        vaug[kh, :, HEAD_DIM:] = ones
        vcaug[kh, :, 0:HEAD_DIM] = vc_ref[0, :, _head_cols(kh)]
        vcaug[kh, :, HEAD_DIM:] = ones[0:n_ctx]

    r = lax.broadcasted_iota(jnp.int32, (Q_ROWS, WINDOW), 0) % WINDOW
    c = lax.broadcasted_iota(jnp.int32, (Q_ROWS, WINDOW), 1)
    neg = jnp.float32(-jnp.inf)
    bias_prev = jnp.where(c >= r, 0.0, neg).astype(F32)
    bias_next = jnp.where(c <= r, 0.0, neg).astype(F32)
    nt = (((1,), (1,)), ((), ()))

    def scores(j, kh, s_ref):
        base = pl.multiple_of(j * WINDOW, WINDOW)
        n = i * nb + j
        q3 = jnp.concatenate([q_ref[0, pl.ds(base, WINDOW), _head_cols(kh * GQA_GROUP + g)]
                              for g in range(GQA_GROUP)], axis=0)
        s_ctx = lax.dot_general(q3, kc_ref[0, :, _head_cols(kh)], nt, preferred_element_type=F32)
        s_win = lax.dot_general(q3, kext[pl.ds(base, 3 * WINDOW), _head_cols(kh)], nt,
                                preferred_element_type=F32)
        s_ref[:, 0:n_ctx] = s_ctx
        s_ref[:, n_ctx:n_ctx + WINDOW] = s_win[:, 0:WINDOW] + jnp.where(n > 0, bias_prev, neg)
        s_ref[:, n_ctx + WINDOW:n_ctx + 2 * WINDOW] = s_win[:, WINDOW:2 * WINDOW]
        s_ref[:, n_ctx + 2 * WINDOW:] = s_win[:, 2 * WINDOW:] + jnp.where(n < nblk_total - 1, bias_next, neg)

    def softmax(kh, s_ref, p_ref, d_ref):
        ntile = s_ref.shape[1] // LANES
        tiles = [s_ref[:, t * LANES:(t + 1) * LANES] for t in range(ntile)]
        mx = tiles[0]
        for t in range(1, ntile):
            mx = jnp.maximum(mx, tiles[t])
        sink = jnp.concatenate([jnp.full((WINDOW, 1), sink_ref[kh * GQA_GROUP + g] * math.log2(math.e), F32)
                                for g in range(GQA_GROUP)], axis=0)
        m = jnp.maximum(jnp.max(mx, axis=1, keepdims=True), sink)
        for t in range(ntile):
            p_ref[:, t * LANES:(t + 1) * LANES] = jnp.exp2(tiles[t] - m).astype(BF16)
        d_ref[...] = jnp.broadcast_to(jnp.exp2(sink - m), (Q_ROWS, HEAD_DIM))

    def values(j, kh, p_ref, d_ref):
        base = pl.multiple_of(j * WINDOW, WINDOW)
        o = jnp.dot(p_ref[:, 0:n_ctx], vcaug[kh], preferred_element_type=F32)
        o = o + jnp.dot(p_ref[:, n_ctx:], vaug[kh, pl.ds(base, 3 * WINDOW), :], preferred_element_type=F32)
        res = o[:, 0:HEAD_DIM] * (1.0 / (o[:, HEAD_DIM:] + d_ref[...]))
        for g in range(GQA_GROUP):
            o_ref[0, pl.ds(base, WINDOW), _head_cols(kh * GQA_GROUP + g)] = (
                res[g * WINDOW:(g + 1) * WINDOW].astype(o_ref.dtype))

    def tick(j, kh, do_scores=True, do_softmax=True, do_values=True):
        par = kh % 2
        if do_scores:
            scores(j, kh, s_scr[par])
        if do_softmax:
            khb = (kh - 1) % N_KV_HEADS
            softmax(khb, s_scr[1 - par], p_scr[1 - par], d_scr[1 - par])
        if do_values:
            jc, khc = (j, kh - 2) if kh >= 2 else (j - 1, kh + 2)
            values(jc, khc, p_scr[par], d_scr[par])

    tick(0, 0, do_softmax=False, do_values=False)
    tick(0, 1, do_values=False)
    tick(0, 2)
    tick(0, 3)

    def block(j, carry):
        for kh in range(N_KV_HEADS):
            tick(j, kh)
        return carry

    lax.fori_loop(1, nb, block, 0)
    tick(nb, 0, do_scores=False)
    tick(nb, 1, do_scores=False, do_softmax=False)


def _attention(q, k, v, kc, vc, sink):
    b, s, _ = q.shape
    c = kc.shape[1]
    steps = s // ATTN_TQ
    per = ATTN_TQ // WINDOW
    nblk = s // WINDOW
    main = lambda width: pl.BlockSpec((1, ATTN_TQ, width), lambda bi, i: (bi, i, 0))
    prev = pl.BlockSpec((1, WINDOW, KV_WIDTH), lambda bi, i: (bi, jnp.maximum(i * per - 1, 0), 0))
    nxt = pl.BlockSpec((1, WINDOW, KV_WIDTH), lambda bi, i: (bi, jnp.minimum((i + 1) * per, nblk - 1), 0))
    ctx = pl.BlockSpec((1, c, KV_WIDTH), lambda bi, i: (bi, 0, 0))
    return pl.pallas_call(
        _attn_kernel,
        grid=(b, steps),
        in_specs=[pl.BlockSpec(memory_space=pltpu.SMEM), main(ATTN_WIDTH),
                  main(KV_WIDTH), prev, nxt, main(KV_WIDTH), prev, nxt, ctx, ctx],
        out_specs=main(ATTN_WIDTH),
        out_shape=jax.ShapeDtypeStruct((b, s, ATTN_WIDTH), BF16),
        scratch_shapes=[pltpu.VMEM((ATTN_TQ + 2 * WINDOW, KV_WIDTH), BF16),
                        pltpu.VMEM((N_KV_HEADS, ATTN_TQ + 2 * WINDOW, 2 * HEAD_DIM), BF16),
                        pltpu.VMEM((N_KV_HEADS, c, 2 * HEAD_DIM), BF16)]
                       + [pltpu.VMEM((Q_ROWS, c + 3 * WINDOW), F32)] * 2
                       + [pltpu.VMEM((Q_ROWS, c + 3 * WINDOW), BF16)] * 2
                       + [pltpu.VMEM((Q_ROWS, HEAD_DIM), F32)] * 2,
        compiler_params=_compiler_params(("parallel", "parallel")),
        name="window_attn",
    )(sink, q, k, k, k, v, v, v, kc, vc)


FFT_SUB = SUBLANES
FFT_BLOCKS = 4
FFT_STEP = FFT_BLOCKS * FFT_SUB
FFT_STEPS = FFT_RADIX // FFT_STEP


def _fourier_kernel(*refs):
    nblk = FFT_BLOCKS
    x_refs = refs[:nblk]
    s1_ref, s2_ref, chan_ref, wf_ref, tw_ref, o_ref, hr_scr, hi_scr, w_scr = refs[nblk:nblk + 9]
    scr = refs[nblk + 9:]
    yr_scr, yi_scr, z_scr = scr[0:nblk], scr[nblk:2 * nblk], scr[2 * nblk:3 * nblk]
    r = FFT_RADIX
    t = pl.program_id(2)

    @pl.when(t == 0)
    def _():
        w = jnp.dot(chan_ref[...], wf_ref[0].astype(BF16), preferred_element_type=F32)
        w_scr[:, 0:r] = w[0:r].astype(BF16)
        w_scr[:, r:2 * r] = w[r:2 * r].astype(BF16)


    def channel_stage(i):
        x2 = x_refs[i][...].reshape(r * FFT_SUB, LANES).astype(BF16)
        y = jnp.dot(x2, w_scr[...], preferred_element_type=F32)
        yr_scr[i][...] = y[:, 0:r]
        yi_scr[i][...] = y[:, r:2 * r]

    def n2_stage(i):
        top = jnp.concatenate([yr_scr[i][pl.ds(j, r, stride=FFT_SUB), :].astype(BF16) for j in range(FFT_SUB)], axis=1)
        bot = jnp.concatenate([yi_scr[i][pl.ds(j, r, stride=FFT_SUB), :].astype(BF16) for j in range(FFT_SUB)], axis=1)
        h = jnp.dot(s1_ref[...], jnp.concatenate([top, bot], axis=0), preferred_element_type=F32)
        for j in range(FFT_SUB):
            n1 = (nblk * t + i) * FFT_SUB + j
            row = pl.multiple_of(n1 * FFT_PITCH, SUBLANES)
            hr_scr[pl.ds(row, r), :] = h[0:r, j * LANES:(j + 1) * LANES]
            hi_scr[pl.ds(row, r), :] = h[r:2 * r, j * LANES:(j + 1) * LANES]

    @pl.when(t < FFT_STEPS)
    def _():
        for i in range(nblk + 1):
            if i < nblk:
                channel_stage(i)
            if i > 0:
                n2_stage(i - 1)

    def twiddle_stage(i):
        tops, bots = [], []
        for j in range(i * FFT_SUB, (i + 1) * FFT_SUB):
            v = (t - FFT_STEPS) * FFT_STEP + j
            hr = hr_scr[pl.ds(v, r, stride=FFT_PITCH), :].astype(BF16)
            hi = hi_scr[pl.ds(v, r, stride=FFT_PITCH), :].astype(BF16)
            tc = tw_ref[0, v]
            ts = tw_ref[1, v]
            tops.append(hr * tc + hi * ts)
            bots.append(hi * tc - hr * ts)
        return jnp.concatenate([jnp.concatenate(tops, axis=1), jnp.concatenate(bots, axis=1)], axis=0)

    def n1_stage(i, rhs):
        z = jnp.dot(s2_ref[...], rhs, preferred_element_type=F32)
        for j in range(FFT_SUB):
            z_scr[i][pl.ds(j, r, stride=FFT_SUB), :] = z[:, j * LANES:(j + 1) * LANES]
        o_ref[:, i * FFT_SUB:(i + 1) * FFT_SUB, :] = z_scr[i][...].reshape(r, FFT_SUB, LANES)

    @pl.when(t >= FFT_STEPS)
    def _():
        rhs = [None] * nblk
        for i in range(nblk + 1):
            if i < nblk:
                rhs[i] = twiddle_stage(i)
            if i > 0:
                n1_stage(i - 1, rhs[i - 1])


def _fourier_mix(f, w_f):
    b, s, _ = f.shape
    r = FFT_RADIX
    assert s == r * r
    s1, s2, chan, tw = _dft_tables()
    f5 = f.reshape(b, r, r // FFT_SUB, FFT_SUB, FOURIER_WIDTH)
    last = r // FFT_SUB - 1
    x_spec = lambda i: pl.BlockSpec(
        (None, r, None, FFT_SUB, LANES), lambda bi, g, t: (bi, 0, jnp.minimum(FFT_BLOCKS * t + i, last), 0, g))
    o_spec = pl.BlockSpec((None, r, None, FFT_STEP, LANES),
                          lambda bi, g, t: (bi, 0, jnp.maximum(t - FFT_STEPS, 0), 0, g))
    const = lambda shape: _resident(shape, lambda bi, g, t: (0,) * len(shape))
    z5 = pl.pallas_call(
        _fourier_kernel,
        grid=(b, FOURIER_GROUPS, 2 * FFT_STEPS),
        in_specs=[x_spec(i) for i in range(FFT_BLOCKS)]
                 + [const((2 * r, 2 * r)), const((r, 2 * r)), const((2 * r, r)),
                    pl.BlockSpec((1, r, r), lambda bi, g, t: (g, 0, 0)), const((2, r, r, LANES))],
        out_specs=o_spec,
        out_shape=jax.ShapeDtypeStruct((b, r, FFT_STEPS, FFT_STEP, FOURIER_WIDTH), F32),
        scratch_shapes=[pltpu.VMEM((r * FFT_PITCH, LANES), F32), pltpu.VMEM((r * FFT_PITCH, LANES), F32),
                        pltpu.VMEM((r, 2 * r), BF16)]
                       + [pltpu.VMEM((r * FFT_SUB, LANES), F32)] * (3 * FFT_BLOCKS),
        compiler_params=_compiler_params(("parallel", "parallel", "arbitrary")),
        name="fourier_mix",
    )(*([f5] * FFT_BLOCKS), *(jnp.asarray(a).astype(BF16) for a in (s1, s2, chan)), w_f,
      jnp.asarray(tw).astype(BF16))
    return z5.reshape(b, s, FOURIER_WIDTH)


def _outproj0_kernel(a_ref, zf_ref, sz_ref, x_ref, gate_ref, w_ref, o_ref):
    ya = a_ref[0] * sz_ref[0, :, 0:ATTN_WIDTH]
    yf = zf_ref[0].astype(BF16) * sz_ref[0, :, ATTN_WIDTH:]
    acc = jnp.dot(ya, w_ref[0:ATTN_WIDTH, :], preferred_element_type=F32)
    acc = acc + jnp.dot(yf, w_ref[ATTN_WIDTH:, :], preferred_element_type=F32)
    o_ref[0] = x_ref[0] + gate_ref[0] * acc


def _outproj0(attn, zf, sz, x, mod, w_out, tm=512):
    b, s, d = x.shape
    tok = lambda width: pl.BlockSpec((1, tm, width), lambda bi, i: (bi, i, 0))
    return pl.pallas_call(
        _outproj0_kernel,
        grid=(b, s // tm),
        in_specs=[tok(ATTN_WIDTH), tok(FOURIER_WIDTH), tok(d), tok(d), _mod_spec(d, MOD_GATE),
                  _resident((d, d), lambda bi, i: (0, 0))],
        out_specs=tok(d),
        out_shape=jax.ShapeDtypeStruct((b, s, d), F32),
        compiler_params=_compiler_params(("parallel", "parallel")),
        name="outproj0",
    )(attn, zf, sz, x, mod, w_out)


IN1_CHUNK = 512


def _inproj1_kernel(x_ref, g_ref, scale_ref, shift_ref, w_ref, u_ref, sz_ref, h_scr):
    tm = x_ref.shape[1]
    row_sets = [slice(r0, r0 + ROW_SET) for r0 in range(0, tm, ROW_SET)]

    def normalize(rows):
        h_scr[rows, :] = _norm_mod(x_ref[0, rows, :], g_ref[...], scale_ref[0], shift_ref[0]).astype(BF16)

    normalize(row_sets[0])
    for c0 in range(0, 2 * D_MODEL, IN1_CHUNK):
        for si, rows in enumerate(row_sets):
            acc = jnp.dot(h_scr[rows, :], w_ref[:, c0:c0 + IN1_CHUNK], preferred_element_type=F32)
            if c0 == 0 and si + 1 < len(row_sets):
                normalize(row_sets[si + 1])
            if c0 < D_MODEL:
                u_ref[0, rows, c0:c0 + IN1_CHUNK] = acc.astype(u_ref.dtype)
            else:
                sz_ref[0, rows, c0 - D_MODEL:c0 - D_MODEL + IN1_CHUNK] = _silu(acc).astype(sz_ref.dtype)


def _inproj1(x, norm_g, mod, w_in, tm=1024):
    b, s, d = x.shape
    tok = pl.BlockSpec((1, tm, d), lambda bi, i: (bi, i, 0))
    return pl.pallas_call(
        _inproj1_kernel,
        grid=(b, s // tm),
        in_specs=[tok, _resident((1, d), lambda bi, i: (0, 0)), _mod_spec(d, MOD_SCALE), _mod_spec(d, MOD_SHIFT),
                  _resident((d, 2 * d), lambda bi, i: (0, 0))],
        out_specs=[tok, tok],
        out_shape=[jax.ShapeDtypeStruct((b, s, d), BF16)] * 2,
        scratch_shapes=[pltpu.VMEM((tm, d), BF16)],
        compiler_params=_compiler_params(("parallel", "parallel")),
        name="inproj1",
    )(x, norm_g.reshape(1, d), mod, mod, w_in)


POOL_HALO = BF16_SUBLANES
POOL_SUB = 128
POOL_PAD = 64


@functools.lru_cache(maxsize=None)
def _pool_band():
    p = np.arange(POOL_SUB)[:, None]
    e = np.arange(POOL_SUB + 2 * POOL_PAD)[None, :] - POOL_PAD
    return np.stack([((e >= p - w // 2) & (e <= p - w // 2 + w - 1)) for w in POOL_WINDOWS]).astype(np.float32)


def _pool_out_kernel(u_ref, up_ref, un_ref, sz_ref, x_ref, gate_ref, ls_ref, fn_ref,
                     inv_first_ref, inv_last_ref, inv_mid_ref, band_ref, wg_ref, wo_ref, o_ref, ext_scr, y_scr):
    i = pl.program_id(1)
    last = pl.num_programs(1) - 1
    tm = u_ref.shape[1]
    pad = POOL_PAD
    zeros = jnp.zeros((pad - POOL_HALO, u_ref.shape[2]), BF16)
    ext_scr[0:pad - POOL_HALO] = zeros
    ext_scr[pad - POOL_HALO:pad] = jnp.where(i > 0, up_ref[0], jnp.zeros_like(up_ref[0]))
    ext_scr[pad:pad + tm] = u_ref[0]
    ext_scr[pad + tm:pad + tm + POOL_HALO] = jnp.where(i < last, un_ref[0], jnp.zeros_like(un_ref[0]))
    ext_scr[pad + tm + POOL_HALO:] = zeros
    for g, w in enumerate(POOL_WINDOWS):
        cols = slice(g * POOL_GROUP_DIM, (g + 1) * POOL_GROUP_DIM)
        uc = u_ref[0, :, cols].astype(F32)
        wsum = jnp.concatenate(
            [jnp.dot(band_ref[g], ext_scr[r0:r0 + POOL_SUB + 2 * pad, cols], preferred_element_type=F32)
             for r0 in range(0, tm, POOL_SUB)], axis=0)
        inv_mid = inv_mid_ref[:, cols]
        inv_head = jnp.where(i == 0, inv_first_ref[:, cols], inv_mid)
        inv_tail = jnp.where(i == last, inv_last_ref[:, cols], inv_mid)
        mean = jnp.concatenate([wsum[0:SUBLANES] * inv_head,
                                wsum[SUBLANES:tm - SUBLANES] * inv_mid,
                                wsum[tm - SUBLANES:] * inv_tail], axis=0)
        pooled = (mean - uc).astype(BF16)
        y_scr[:, cols] = jnp.dot(pooled, wg_ref[g], preferred_element_type=F32)

    yz = (y_scr[...] * ls_ref[...]).astype(BF16) * sz_ref[0]
    acc = jnp.dot(yz, wo_ref[...], preferred_element_type=F32)
    x2 = x_ref[0] + gate_ref[0] * acc
    ms = jnp.mean(x2 * x2, axis=-1, keepdims=True)
    o_ref[0] = x2 * lax.rsqrt(ms + RMS_EPS) * fn_ref[...]


def _pool_out(u, sz, x, mod, layer_scale, final_norm, w_grp, w_out, tm=512):
    b, s, d = x.shape
    per = tm // POOL_HALO
    nh = s // POOL_HALO
    inv_first, inv_last, inv_mid = _pool_edge_tables(s)
    tok = pl.BlockSpec((1, tm, d), lambda bi, i: (bi, i, 0))
    prev = pl.BlockSpec((1, POOL_HALO, d), lambda bi, i: (bi, jnp.maximum(i * per - 1, 0), 0))
    nxt = pl.BlockSpec((1, POOL_HALO, d), lambda bi, i: (bi, jnp.minimum((i + 1) * per, nh - 1), 0))
    row = lambda rows: _resident((rows, d), lambda bi, i: (0, 0))
    return pl.pallas_call(
        _pool_out_kernel,
        grid=(b, s // tm),
        in_specs=[tok, prev, nxt, tok, tok, _mod_spec(d, MOD_GATE),
                  row(1), row(1), row(SUBLANES), row(SUBLANES), row(1),
                  _resident((POOL_GROUPS, POOL_SUB, POOL_SUB + 2 * POOL_PAD), lambda bi, i: (0, 0, 0)),
                  _resident((POOL_GROUPS, POOL_GROUP_DIM, POOL_GROUP_DIM), lambda bi, i: (0, 0, 0)),
                  _resident((d, d), lambda bi, i: (0, 0))],
        out_specs=tok,
        out_shape=jax.ShapeDtypeStruct((b, s, d), F32),
        scratch_shapes=[pltpu.VMEM((tm + 2 * POOL_PAD, d), BF16), pltpu.VMEM((tm, d), F32)],
        compiler_params=_compiler_params(("parallel", "parallel")),
        name="pool_out",
    )(u, u, u, sz, x, mod, layer_scale.reshape(1, d), final_norm.reshape(1, d),
      jnp.asarray(inv_first), jnp.asarray(inv_last), jnp.asarray(inv_mid), jnp.asarray(_pool_band()).astype(BF16),
      w_grp, w_out)


def kernel(x, c, ctx, c_ctx, l0_norm, l0_w_mod, l0_b_mod, l0_w_in, l0_sink, l0_w_f, l0_w_out,
           l1_norm, l1_w_mod, l1_b_mod, l1_w_in, l1_w_grp, l1_scale, l1_w_out, final_norm):
    b, s, d = x.shape
    cond = jnp.concatenate([c, c_ctx[None, :], jnp.zeros((COND_ROWS - b - 1, d), F32)], axis=0)
    m0 = _modulation(cond, l0_w_mod, l0_b_mod).reshape(COND_ROWS, 1, 3 * d)
    m1 = _modulation(cond, l1_w_mod, l1_b_mod).reshape(COND_ROWS, 1, 3 * d)

    w_in0 = l0_w_in.astype(BF16)
    cos, sin = _rope_tables(s)
    side = [l0_w_out, l1_w_in, l1_w_grp.reshape(POOL_GROUPS * POOL_GROUP_DIM, POOL_GROUP_DIM), l1_w_out]
    (q, k, v, f, sz0), (w_out0, w_in1, w_grp, w_out1) = _inproj0(
        x, l0_norm, m0, w_in0, jnp.asarray(cos), jnp.asarray(sin), side)
    kc, vc = _ctx_kv(ctx, l0_norm, m0, b, w_in0)
    attn = _attention(q, k, v, kc, vc, l0_sink)
    zf = _fourier_mix(f, l0_w_f)
    x1 = _outproj0(attn, zf, sz0, x, m0, w_out0)

    u, sz1 = _inproj1(x1, l1_norm, m1, w_in1)
    return _pool_out(u, sz1, x1, m1, l1_scale, final_norm, w_grp.reshape(l1_w_grp.shape), w_out1)
```

```python
import functools
import math

import numpy as np
import jax
import jax.numpy as jnp
from jax import lax
from jax.experimental import pallas as pl
from jax.experimental.pallas import tpu as pltpu

F32 = jnp.float32
BF16 = jnp.bfloat16

D_MODEL = 2048
GRID_W = 64
HEAD_DIM = 128
N_Q_HEADS = 12
N_KV_HEADS = 4
GQA_GROUP = N_Q_HEADS // N_KV_HEADS
WINDOW = 128
ATTN_WIDTH = N_Q_HEADS * HEAD_DIM
KV_WIDTH = N_KV_HEADS * HEAD_DIM
FOURIER_WIDTH = D_MODEL - ATTN_WIDTH
FOURIER_GROUPS = 4
FOURIER_GROUP_DIM = FOURIER_WIDTH // FOURIER_GROUPS
IN0_WIDTH = ATTN_WIDTH + 2 * KV_WIDTH + FOURIER_WIDTH + D_MODEL
POOL_WINDOWS = (2, 4, 8, 16)
POOL_GROUPS = len(POOL_WINDOWS)
POOL_GROUP_DIM = D_MODEL // POOL_GROUPS
ROPE_BASE = 10000.0
ROPE_PAIRS = HEAD_DIM // 4
RMS_EPS = 1e-6

LANES = 128
SUBLANES = 8
BF16_SUBLANES = 16
VMEM_LIMIT_BYTES = 56 * 1024 * 1024

FFT_RADIX = 128
FFT_PITCH = FFT_RADIX + SUBLANES

COND_ROWS = 8
MOD_TN = 1536


ROW_SET = 256


def _silu(x):
    hx = 0.5 * x
    return hx + hx * jnp.tanh(hx)


def _compiler_params(semantics):
    return pltpu.CompilerParams(dimension_semantics=semantics, vmem_limit_bytes=VMEM_LIMIT_BYTES)


def _resident(block_shape, index_map):
    return pl.BlockSpec(block_shape, index_map, pipeline_mode=pl.Buffered(1))


@functools.lru_cache(maxsize=None)
def _rope_tables(seq):
    pos = np.arange(seq)
    row = (pos // GRID_W).astype(np.float64)
    col = (pos % GRID_W).astype(np.float64)
    inv = 1.0 / (ROPE_BASE ** (np.arange(ROPE_PAIRS, dtype=np.float64) / ROPE_PAIRS))
    ar, ac = row[:, None] * inv, col[:, None] * inv
    cos = np.concatenate([np.cos(ar), np.cos(ar), np.cos(ac), np.cos(ac)], axis=1)
    sin = np.concatenate([-np.sin(ar), np.sin(ar), -np.sin(ac), np.sin(ac)], axis=1)
    return cos.astype(np.float32), sin.astype(np.float32)


@functools.lru_cache(maxsize=None)
def _dft_tables():
    r = FFT_RADIX
    idx = np.arange(r)
    ang = 2.0 * np.pi * np.outer(idx, idx) / r
    c, s = np.cos(ang) / math.sqrt(r), np.sin(ang) / math.sqrt(r)
    stage1 = np.block([[c, s], [-s, c]])
    stage2 = np.concatenate([c, s], axis=1)
    chan = np.concatenate([c, -s], axis=0)
    ang2 = 2.0 * np.pi * ((idx[:, None] * idx[None, :]) % (r * r)) / (r * r)
    tw = np.stack([np.cos(ang2), np.sin(ang2)])[..., None] * np.ones(LANES)
    return (stage1.astype(np.float32), stage2.astype(np.float32), chan.astype(np.float32),
            tw.astype(np.float32))


@functools.lru_cache(maxsize=None)
def _pool_edge_tables(seq):
    first = np.zeros((SUBLANES, D_MODEL), np.float64)
    last = np.zeros((SUBLANES, D_MODEL), np.float64)
    for g, w in enumerate(POOL_WINDOWS):
        for r in range(SUBLANES):
            for tab, t in ((first, r), (last, seq - SUBLANES + r)):
                lo = min(max(t - w // 2, 0), seq)
                hi = min(max(t - w // 2 + w, 0), seq)
                tab[r, g * POOL_GROUP_DIM:(g + 1) * POOL_GROUP_DIM] = 1.0 / (hi - lo)
    interior = np.repeat(1.0 / np.array(POOL_WINDOWS, np.float64), POOL_GROUP_DIM)[None, :]
    return first.astype(np.float32), last.astype(np.float32), interior.astype(np.float32)


def _mod_kernel(cond_ref, w_ref, b_ref, o_ref):
    a = _silu(cond_ref[...]).astype(BF16)
    o_ref[:, 0, :] = jnp.dot(a, w_ref[...].astype(BF16), preferred_element_type=F32) + b_ref[...]


def _modulation(cond, w_mod, b_mod):
    d, n = w_mod.shape
    tn = MOD_TN
    return pl.pallas_call(
        _mod_kernel,
        grid=(n // tn,),
        in_specs=[
            pl.BlockSpec((COND_ROWS, d), lambda j: (0, 0)),
            pl.BlockSpec((d, tn), lambda j: (0, j)),
            pl.BlockSpec((1, tn), lambda j: (0, j)),
        ],
        out_specs=pl.BlockSpec((COND_ROWS, 1, tn), lambda j: (0, 0, j)),
        out_shape=jax.ShapeDtypeStruct((COND_ROWS, 1, n), F32),
        compiler_params=_compiler_params(("arbitrary",)),
        name="modulation",
    )(cond, w_mod, b_mod.reshape(1, n))


MOD_SHIFT, MOD_SCALE, MOD_GATE = 0, 1, 2


def _mod_spec(d, k, row=None):
    if row is None:
        return pl.BlockSpec((1, 1, d), lambda bi, *_: (bi, 0, k))
    return pl.BlockSpec((1, 1, d), lambda bi, *_: (row, 0, k))


def _norm_mod(x, g, scale, shift):
    ms = jnp.mean(x * x, axis=-1, keepdims=True)
    return x * lax.rsqrt(ms + RMS_EPS) * g * (1.0 + scale) + shift


IN0_CHUNK = 512


def _rope(x, cos, sin, partner_is_right):
    rot = jnp.where(partner_is_right, pltpu.roll(x, HEAD_DIM - ROPE_PAIRS, 1), pltpu.roll(x, ROPE_PAIRS, 1))
    return x * cos + rot * sin


def _inproj0_kernel(n_side, x_ref, g_ref, scale_ref, shift_ref, w_ref, cos_ref, sin_ref, *rest):
    side_in, (q_ref, k_ref, v_ref, f_ref, sz_ref) = rest[:n_side], rest[n_side:n_side + 5]
    side_out, h_scr = rest[n_side + 5:2 * n_side + 5], rest[2 * n_side + 5]
    tm = x_ref.shape[1]
    q_scale = HEAD_DIM ** -0.5 * math.log2(math.e)
    lane = lax.broadcasted_iota(jnp.int32, (ROW_SET, HEAD_DIM), 1)
    partner_is_right = (lane % (2 * ROPE_PAIRS)) < ROPE_PAIRS
    heads_per_chunk = IN0_CHUNK // HEAD_DIM
    row_sets = [slice(r0, r0 + ROW_SET) for r0 in range(0, tm, ROW_SET)]

    def normalize(rows):
        h_scr[rows, :] = _norm_mod(x_ref[0, rows, :], g_ref[...], scale_ref[0], shift_ref[0]).astype(BF16)

    normalize(row_sets[0])
    gate_start = IN0_WIDTH - D_MODEL
    order = list(range(gate_start, IN0_WIDTH, IN0_CHUNK)) + list(range(0, gate_start, IN0_CHUNK))
    for ci, c0 in enumerate(order):
        for si, rows in enumerate(row_sets):
            acc = jnp.dot(h_scr[rows, :], w_ref[:, c0:c0 + IN0_CHUNK], preferred_element_type=F32)
            if ci == 0 and si + 1 < len(row_sets):
                normalize(row_sets[si + 1])
            if ci == 1 and si == 0:
                for src, dst in zip(side_in, side_out):
                    dst[...] = src[...].astype(dst.dtype)
            if c0 < ATTN_WIDTH + KV_WIDTH:
                is_q = c0 < ATTN_WIDTH
                dst, off = (q_ref, c0) if is_q else (k_ref, c0 - ATTN_WIDTH)
                cs, sn = cos_ref[rows, :], sin_ref[rows, :]
                if is_q:
                    cs, sn = cs * q_scale, sn * q_scale
                for j in range(heads_per_chunk):
                    xh = acc[:, j * HEAD_DIM:(j + 1) * HEAD_DIM]
                    lo = off + j * HEAD_DIM
                    dst[0, rows, lo:lo + HEAD_DIM] = _rope(xh, cs, sn, partner_is_right).astype(dst.dtype)
            elif c0 < ATTN_WIDTH + 2 * KV_WIDTH:
                v_ref[0, rows, :] = acc.astype(v_ref.dtype)
            elif c0 < ATTN_WIDTH + 2 * KV_WIDTH + FOURIER_WIDTH:
                f_ref[0, rows, :] = acc
            else:
                off = c0 - (IN0_WIDTH - D_MODEL)
                sz_ref[0, rows, off:off + IN0_CHUNK] = _silu(acc).astype(sz_ref.dtype)


def _inproj0(x, norm_g, mod, w_in, cos, sin, side_weights, tm=512):
    b, s, d = x.shape
    steps = s // tm
    tok = lambda width: pl.BlockSpec((1, tm, width), lambda bi, i: (bi, i, 0))
    tab = pl.BlockSpec((tm, HEAD_DIM), lambda bi, i: (i, 0))
    slab = lambda w: pl.BlockSpec((w.shape[0] // (b * steps), w.shape[1]), lambda bi, i: (bi * steps + i, 0))
    for w in side_weights:
        assert w.shape[0] % (b * steps * BF16_SUBLANES) == 0, w.shape
    outs = pl.pallas_call(
        functools.partial(_inproj0_kernel, len(side_weights)),
        grid=(b, steps),
        in_specs=[tok(d), _resident((1, d), lambda bi, i: (0, 0)), _mod_spec(d, MOD_SCALE), _mod_spec(d, MOD_SHIFT),
                  _resident((d, IN0_WIDTH), lambda bi, i: (0, 0)), tab, tab] + [slab(w) for w in side_weights],
        out_specs=[tok(ATTN_WIDTH), tok(KV_WIDTH), tok(KV_WIDTH), tok(FOURIER_WIDTH), tok(D_MODEL)]
                  + [slab(w) for w in side_weights],
        out_shape=[
            jax.ShapeDtypeStruct((b, s, ATTN_WIDTH), BF16),
            jax.ShapeDtypeStruct((b, s, KV_WIDTH), BF16),
            jax.ShapeDtypeStruct((b, s, KV_WIDTH), BF16),
            jax.ShapeDtypeStruct((b, s, FOURIER_WIDTH), F32),
            jax.ShapeDtypeStruct((b, s, D_MODEL), BF16),
        ] + [jax.ShapeDtypeStruct(w.shape, BF16) for w in side_weights],
        scratch_shapes=[pltpu.VMEM((tm, d), BF16)],
        compiler_params=_compiler_params(("parallel", "parallel")),
        name="inproj0",
    )(x, norm_g.reshape(1, d), mod, mod, w_in, cos, sin, *side_weights)
    return outs[:5], outs[5:]


CTX_KSLAB = 256


def _ctx_kv_kernel(c_ref, g_ref, scale_ref, shift_ref, w_ref, wb_ref, k_ref, v_ref, h_scr, acc_scr):
    ks = pl.program_id(0)
    nb, c, d = c_ref.shape
    kv_cols = slice(ATTN_WIDTH, ATTN_WIDTH + 2 * KV_WIDTH)

    @pl.when(ks == 0)
    def _():
        for bi in range(nb):
            h = _norm_mod(c_ref[bi], g_ref[...], scale_ref[0], shift_ref[0]).astype(BF16)
            for j in range(d // CTX_KSLAB):
                h_scr[j, bi * c:(bi + 1) * c, :] = h[:, j * CTX_KSLAB:(j + 1) * CTX_KSLAB]
        acc_scr[...] = jnp.zeros_like(acc_scr)

    wb = w_ref[...].astype(BF16)
    wb_ref[...] = wb
    acc_scr[...] += jnp.dot(h_scr[ks], wb[:, kv_cols], preferred_element_type=F32)

    @pl.when(ks == pl.num_programs(0) - 1)
    def _():
        for bi in range(nb):
            k_ref[bi] = acc_scr[bi * c:(bi + 1) * c, 0:KV_WIDTH].astype(k_ref.dtype)
            v_ref[bi] = acc_scr[bi * c:(bi + 1) * c, KV_WIDTH:].astype(v_ref.dtype)


def _ctx_kv(ctx, norm_g, mod, mod_row, w_in):
    b, c, d = ctx.shape
    n = w_in.shape[1]
    whole = lambda shape: pl.BlockSpec(shape, lambda ks: (0,) * len(shape))
    slab = pl.BlockSpec((CTX_KSLAB, n), lambda ks: (ks, 0))
    return pl.pallas_call(
        _ctx_kv_kernel,
        grid=(d // CTX_KSLAB,),
        in_specs=[whole((b, c, d)), whole((1, d)), _mod_spec(d, MOD_SCALE, mod_row), _mod_spec(d, MOD_SHIFT, mod_row),
                  slab],
        out_specs=[slab, whole((b, c, KV_WIDTH)), whole((b, c, KV_WIDTH))],
        out_shape=[jax.ShapeDtypeStruct(w_in.shape, BF16)] + [jax.ShapeDtypeStruct((b, c, KV_WIDTH), BF16)] * 2,
        scratch_shapes=[pltpu.VMEM((d // CTX_KSLAB, b * c, CTX_KSLAB), BF16),
                        pltpu.VMEM((b * c, 2 * KV_WIDTH), F32)],
        compiler_params=_compiler_params(("arbitrary",)),
        name="ctx_kv",
    )(ctx, norm_g.reshape(1, d), mod, mod, w_in)


ATTN_TQ = 2048
Q_ROWS = GQA_GROUP * WINDOW


def _head_cols(h):
    return slice(h * HEAD_DIM, (h + 1) * HEAD_DIM)


def _attn_kernel(sink_ref, q_ref, k_ref, kp_ref, kn_ref, v_ref, vp_ref, vn_ref, kc_ref, vc_ref,
                 o_ref, kext, vaug, vcaug, s0, s1, p0, p1, d0, d1):
    i = pl.program_id(1)
    nb = ATTN_TQ // WINDOW
    nblk_total = pl.num_programs(1) * nb
    n_ctx = kc_ref.shape[1]
    s_scr, p_scr, d_scr = (s0, s1), (p0, p1), (d0, d1)

    kext[0:WINDOW] = kp_ref[0]
    kext[WINDOW:WINDOW + ATTN_TQ] = k_ref[0]
    kext[WINDOW + ATTN_TQ:] = kn_ref[0]
    ones = jnp.ones((ATTN_TQ + 2 * WINDOW, HEAD_DIM), BF16)
    for kh in range(N_KV_HEADS):
        vaug[kh, 0:WINDOW, 0:HEAD_DIM] = vp_ref[0, :, _head_cols(kh)]
        vaug[kh, WINDOW:WINDOW + ATTN_TQ, 0:HEAD_DIM] = v_ref[0, :, _head_cols(kh)]
        vaug[kh, WINDOW + ATTN_TQ:, 0:HEAD_DIM] = vn_ref[0, :, _head_cols(kh)]
        vaug[kh, :, HEAD_DIM:] = ones
        vcaug[kh, :, 0:HEAD_DIM] = vc_ref[0, :, _head_cols(kh)]
        vcaug[kh, :, HEAD_DIM:] = ones[0:n_ctx]

    r = lax.broadcasted_iota(jnp.int32, (Q_ROWS, WINDOW), 0) % WINDOW
    c = lax.broadcasted_iota(jnp.int32, (Q_ROWS, WINDOW), 1)
    neg = jnp.float32(-jnp.inf)
    bias_prev = jnp.where(c >= r, 0.0, neg).astype(F32)
    bias_next = jnp.where(c <= r, 0.0, neg).astype(F32)
    nt = (((1,), (1,)), ((), ()))

    def scores(j, kh, s_ref):
        base = pl.multiple_of(j * WINDOW, WINDOW)
        n = i * nb + j
        q3 = jnp.concatenate([q_ref[0, pl.ds(base, WINDOW), _head_cols(kh * GQA_GROUP + g)]
                              for g in range(GQA_GROUP)], axis=0)
        s_ctx = lax.dot_general(q3, kc_ref[0, :, _head_cols(kh)], nt, preferred_element_type=F32)
        s_win = lax.dot_general(q3, kext[pl.ds(base, 3 * WINDOW), _head_cols(kh)], nt,
                                preferred_element_type=F32)
        s_ref[:, 0:n_ctx] = s_ctx
        s_ref[:, n_ctx:n_ctx + WINDOW] = s_win[:, 0:WINDOW] + jnp.where(n > 0, bias_prev, neg)
        s_ref[:, n_ctx + WINDOW:n_ctx + 2 * WINDOW] = s_win[:, WINDOW:2 * WINDOW]
        s_ref[:, n_ctx + 2 * WINDOW:] = s_win[:, 2 * WINDOW:] + jnp.where(n < nblk_total - 1, bias_next, neg)

    def softmax(kh, s_ref, p_ref, d_ref):
        ntile = s_ref.shape[1] // LANES
        tiles = [s_ref[:, t * LANES:(t + 1) * LANES] for t in range(ntile)]
        mx = tiles[0]
        for t in range(1, ntile):
            mx = jnp.maximum(mx, tiles[t])
        sink = jnp.concatenate([jnp.full((WINDOW, 1), sink_ref[kh * GQA_GROUP + g] * math.log2(math.e), F32)
                                for g in range(GQA_GROUP)], axis=0)
        m = jnp.maximum(jnp.max(mx, axis=1, keepdims=True), sink)
        for t in range(ntile):
            p_ref[:, t * LANES:(t + 1) * LANES] = jnp.exp2(tiles[t] - m).astype(BF16)
        d_ref[...] = jnp.broadcast_to(jnp.exp2(sink - m), (Q_ROWS, HEAD_DIM))

    def values(j, kh, p_ref, d_ref):
        base = pl.multiple_of(j * WINDOW, WINDOW)
        o = jnp.dot(p_ref[:, 0:n_ctx], vcaug[kh], preferred_element_type=F32)
        o = o + jnp.dot(p_ref[:, n_ctx:], vaug[kh, pl.ds(base, 3 * WINDOW), :], preferred_element_type=F32)
        res = o[:, 0:HEAD_DIM] * (1.0 / (o[:, HEAD_DIM:] + d_ref[...]))
        for g in range(GQA_GROUP):
            o_ref[0, pl.ds(base, WINDOW), _head_cols(kh * GQA_GROUP + g)] = (
                res[g * WINDOW:(g + 1) * WINDOW].astype(o_ref.dtype))

    def tick(j, kh, do_scores=True, do_softmax=True, do_values=True):
        par = kh % 2
        if do_scores:
            scores(j, kh, s_scr[par])
        if do_softmax:
            khb = (kh - 1) % N_KV_HEADS
            softmax(khb, s_scr[1 - par], p_scr[1 - par], d_scr[1 - par])
        if do_values:
            jc, khc = (j, kh - 2) if kh >= 2 else (j - 1, kh + 2)
            values(jc, khc, p_scr[par], d_scr[par])

    tick(0, 0, do_softmax=False, do_values=False)
    tick(0, 1, do_values=False)
    tick(0, 2)
    tick(0, 3)

    def block(j, carry):
        for kh in range(N_KV_HEADS):
            tick(j, kh)
        return carry

    lax.fori_loop(1, nb, block, 0)
    tick(nb, 0, do_scores=False)
    tick(nb, 1, do_scores=False, do_softmax=False)


def _attention(q, k, v, kc, vc, sink):
    b, s, _ = q.shape
    c = kc.shape[1]
    steps = s // ATTN_TQ
    per = ATTN_TQ // WINDOW
    nblk = s // WINDOW
    main = lambda width: pl.BlockSpec((1, ATTN_TQ, width), lambda bi, i: (bi, i, 0))
    prev = pl.BlockSpec((1, WINDOW, KV_WIDTH), lambda bi, i: (bi, jnp.maximum(i * per - 1, 0), 0))
    nxt = pl.BlockSpec((1, WINDOW, KV_WIDTH), lambda bi, i: (bi, jnp.minimum((i + 1) * per, nblk - 1), 0))
    ctx = pl.BlockSpec((1, c, KV_WIDTH), lambda bi, i: (bi, 0, 0))
    return pl.pallas_call(
        _attn_kernel,
        grid=(b, steps),
        in_specs=[pl.BlockSpec(memory_space=pltpu.SMEM), main(ATTN_WIDTH),
                  main(KV_WIDTH), prev, nxt, main(KV_WIDTH), prev, nxt, ctx, ctx],
        out_specs=main(ATTN_WIDTH),
        out_shape=jax.ShapeDtypeStruct((b, s, ATTN_WIDTH), BF16),
        scratch_shapes=[pltpu.VMEM((ATTN_TQ + 2 * WINDOW, KV_WIDTH), BF16),
                        pltpu.VMEM((N_KV_HEADS, ATTN_TQ + 2 * WINDOW, 2 * HEAD_DIM), BF16),
                        pltpu.VMEM((N_KV_HEADS, c, 2 * HEAD_DIM), BF16)]
                       + [pltpu.VMEM((Q_ROWS, c + 3 * WINDOW), F32)] * 2
                       + [pltpu.VMEM((Q_ROWS, c + 3 * WINDOW), BF16)] * 2
                       + [pltpu.VMEM((Q_ROWS, HEAD_DIM), F32)] * 2,
        compiler_params=_compiler_params(("parallel", "parallel")),
        name="window_attn",
    )(sink, q, k, k, k, v, v, v, kc, vc)


FFT_SUB = SUBLANES
FFT_BLOCKS = 4
FFT_STEP = FFT_BLOCKS * FFT_SUB
FFT_STEPS = FFT_RADIX // FFT_STEP


def _fourier_kernel(*refs):
    nblk = FFT_BLOCKS
    x_refs = refs[:nblk]
    s1_ref, s2_ref, chan_ref, wf_ref, tw_ref, o_ref, hr_scr, hi_scr, w_scr = refs[nblk:nblk + 9]
    scr = refs[nblk + 9:]
    yr_scr, yi_scr, z_scr = scr[0:nblk], scr[nblk:2 * nblk], scr[2 * nblk:3 * nblk]
    r = FFT_RADIX
    t = pl.program_id(2)

    @pl.when(t == 0)
    def _():
        w = jnp.dot(chan_ref[...], wf_ref[0].astype(BF16), preferred_element_type=F32)
        w_scr[:, 0:r] = w[0:r].astype(BF16)
        w_scr[:, r:2 * r] = w[r:2 * r].astype(BF16)


    def channel_stage(i):
        x2 = x_refs[i][...].reshape(r * FFT_SUB, LANES).astype(BF16)
        y = jnp.dot(x2, w_scr[...], preferred_element_type=F32)
        yr_scr[i][...] = y[:, 0:r]
        yi_scr[i][...] = y[:, r:2 * r]

    def n2_stage(i):
        top = jnp.concatenate([yr_scr[i][pl.ds(j, r, stride=FFT_SUB), :].astype(BF16) for j in range(FFT_SUB)], axis=1)
        bot = jnp.concatenate([yi_scr[i][pl.ds(j, r, stride=FFT_SUB), :].astype(BF16) for j in range(FFT_SUB)], axis=1)
        h = jnp.dot(s1_ref[...], jnp.concatenate([top, bot], axis=0), preferred_element_type=F32)
        for j in range(FFT_SUB):
            n1 = (nblk * t + i) * FFT_SUB + j
            row = pl.multiple_of(n1 * FFT_PITCH, SUBLANES)
            hr_scr[pl.ds(row, r), :] = h[0:r, j * LANES:(j + 1) * LANES]
            hi_scr[pl.ds(row, r), :] = h[r:2 * r, j * LANES:(j + 1) * LANES]

    @pl.when(t < FFT_STEPS)
    def _():
        for i in range(nblk + 1):
            if i < nblk:
                channel_stage(i)
            if i > 0:
                n2_stage(i - 1)

    def twiddle_stage(i):
        tops, bots = [], []
        for j in range(i * FFT_SUB, (i + 1) * FFT_SUB):
            v = (t - FFT_STEPS) * FFT_STEP + j
            hr = hr_scr[pl.ds(v, r, stride=FFT_PITCH), :].astype(BF16)
            hi = hi_scr[pl.ds(v, r, stride=FFT_PITCH), :].astype(BF16)
            tc = tw_ref[0, v]
            ts = tw_ref[1, v]
            tops.append(hr * tc + hi * ts)
            bots.append(hi * tc - hr * ts)
        return jnp.concatenate([jnp.concatenate(tops, axis=1), jnp.concatenate(bots, axis=1)], axis=0)

    def n1_stage(i, rhs):
        z = jnp.dot(s2_ref[...], rhs, preferred_element_type=F32)
        for j in range(FFT_SUB):
            z_scr[i][pl.ds(j, r, stride=FFT_SUB), :] = z[:, j * LANES:(j + 1) * LANES]
        o_ref[:, i * FFT_SUB:(i + 1) * FFT_SUB, :] = z_scr[i][...].reshape(r, FFT_SUB, LANES)

    @pl.when(t >= FFT_STEPS)
    def _():
        rhs = [None] * nblk
        for i in range(nblk + 1):
            if i < nblk:
                rhs[i] = twiddle_stage(i)
            if i > 0:
                n1_stage(i - 1, rhs[i - 1])


def _fourier_mix(f, w_f):
    b, s, _ = f.shape
    r = FFT_RADIX
    assert s == r * r
    s1, s2, chan, tw = _dft_tables()
    f5 = f.reshape(b, r, r // FFT_SUB, FFT_SUB, FOURIER_WIDTH)
    last = r // FFT_SUB - 1
    x_spec = lambda i: pl.BlockSpec(
        (None, r, None, FFT_SUB, LANES), lambda bi, g, t: (bi, 0, jnp.minimum(FFT_BLOCKS * t + i, last), 0, g))
    o_spec = pl.BlockSpec((None, r, None, FFT_STEP, LANES),
                          lambda bi, g, t: (bi, 0, jnp.maximum(t - FFT_STEPS, 0), 0, g))
    const = lambda shape: _resident(shape, lambda bi, g, t: (0,) * len(shape))
    z5 = pl.pallas_call(
        _fourier_kernel,
        grid=(b, FOURIER_GROUPS, 2 * FFT_STEPS),
        in_specs=[x_spec(i) for i in range(FFT_BLOCKS)]
                 + [const((2 * r, 2 * r)), const((r, 2 * r)), const((2 * r, r)),
                    pl.BlockSpec((1, r, r), lambda bi, g, t: (g, 0, 0)), const((2, r, r, LANES))],
        out_specs=o_spec,
        out_shape=jax.ShapeDtypeStruct((b, r, FFT_STEPS, FFT_STEP, FOURIER_WIDTH), F32),
        scratch_shapes=[pltpu.VMEM((r * FFT_PITCH, LANES), F32), pltpu.VMEM((r * FFT_PITCH, LANES), F32),
                        pltpu.VMEM((r, 2 * r), BF16)]
                       + [pltpu.VMEM((r * FFT_SUB, LANES), F32)] * (3 * FFT_BLOCKS),
        compiler_params=_compiler_params(("parallel", "parallel", "arbitrary")),
        name="fourier_mix",
    )(*([f5] * FFT_BLOCKS), *(jnp.asarray(a).astype(BF16) for a in (s1, s2, chan)), w_f,
      jnp.asarray(tw).astype(BF16))
    return z5.reshape(b, s, FOURIER_WIDTH)


def _outproj0_kernel(a_ref, zf_ref, sz_ref, x_ref, gate_ref, w_ref, o_ref):
    ya = a_ref[0] * sz_ref[0, :, 0:ATTN_WIDTH]
    yf = zf_ref[0].astype(BF16) * sz_ref[0, :, ATTN_WIDTH:]
    acc = jnp.dot(ya, w_ref[0:ATTN_WIDTH, :], preferred_element_type=F32)
    acc = acc + jnp.dot(yf, w_ref[ATTN_WIDTH:, :], preferred_element_type=F32)
    o_ref[0] = x_ref[0] + gate_ref[0] * acc


def _outproj0(attn, zf, sz, x, mod, w_out, tm=512):
    b, s, d = x.shape
    tok = lambda width: pl.BlockSpec((1, tm, width), lambda bi, i: (bi, i, 0))
    return pl.pallas_call(
        _outproj0_kernel,
        grid=(b, s // tm),
        in_specs=[tok(ATTN_WIDTH), tok(FOURIER_WIDTH), tok(d), tok(d), _mod_spec(d, MOD_GATE),
                  _resident((d, d), lambda bi, i: (0, 0))],
        out_specs=tok(d),
        out_shape=jax.ShapeDtypeStruct((b, s, d), F32),
        compiler_params=_compiler_params(("parallel", "parallel")),
        name="outproj0",
    )(attn, zf, sz, x, mod, w_out)


IN1_CHUNK = 512


def _inproj1_kernel(x_ref, g_ref, scale_ref, shift_ref, w_ref, u_ref, sz_ref, h_scr):
    tm = x_ref.shape[1]
    row_sets = [slice(r0, r0 + ROW_SET) for r0 in range(0, tm, ROW_SET)]

    def normalize(rows):
        h_scr[rows, :] = _norm_mod(x_ref[0, rows, :], g_ref[...], scale_ref[0], shift_ref[0]).astype(BF16)

    normalize(row_sets[0])
    for c0 in range(0, 2 * D_MODEL, IN1_CHUNK):
        for si, rows in enumerate(row_sets):
            acc = jnp.dot(h_scr[rows, :], w_ref[:, c0:c0 + IN1_CHUNK], preferred_element_type=F32)
            if c0 == 0 and si + 1 < len(row_sets):
                normalize(row_sets[si + 1])
            if c0 < D_MODEL:
                u_ref[0, rows, c0:c0 + IN1_CHUNK] = acc.astype(u_ref.dtype)
            else:
                sz_ref[0, rows, c0 - D_MODEL:c0 - D_MODEL + IN1_CHUNK] = _silu(acc).astype(sz_ref.dtype)


def _inproj1(x, norm_g, mod, w_in, tm=1024):
    b, s, d = x.shape
    tok = pl.BlockSpec((1, tm, d), lambda bi, i: (bi, i, 0))
    return pl.pallas_call(
        _inproj1_kernel,
        grid=(b, s // tm),
        in_specs=[tok, _resident((1, d), lambda bi, i: (0, 0)), _mod_spec(d, MOD_SCALE), _mod_spec(d, MOD_SHIFT),
                  _resident((d, 2 * d), lambda bi, i: (0, 0))],
        out_specs=[tok, tok],
        out_shape=[jax.ShapeDtypeStruct((b, s, d), BF16)] * 2,
        scratch_shapes=[pltpu.VMEM((tm, d), BF16)],
        compiler_params=_compiler_params(("parallel", "parallel")),
        name="inproj1",
    )(x, norm_g.reshape(1, d), mod, mod, w_in)


POOL_HALO = BF16_SUBLANES
POOL_SUB = 128
POOL_PAD = 64


@functools.lru_cache(maxsize=None)
def _pool_band():
    p = np.arange(POOL_SUB)[:, None]
    e = np.arange(POOL_SUB + 2 * POOL_PAD)[None, :] - POOL_PAD
    return np.stack([((e >= p - w // 2) & (e <= p - w // 2 + w - 1)) for w in POOL_WINDOWS]).astype(np.float32)


def _pool_out_kernel(u_ref, up_ref, un_ref, sz_ref, x_ref, gate_ref, ls_ref, fn_ref,
                     inv_first_ref, inv_last_ref, inv_mid_ref, band_ref, wg_ref, wo_ref, o_ref, ext_scr, y_scr):
    i = pl.program_id(1)
    last = pl.num_programs(1) - 1
    tm = u_ref.shape[1]
    pad = POOL_PAD
    zeros = jnp.zeros((pad - POOL_HALO, u_ref.shape[2]), BF16)
    ext_scr[0:pad - POOL_HALO] = zeros
    ext_scr[pad - POOL_HALO:pad] = jnp.where(i > 0, up_ref[0], jnp.zeros_like(up_ref[0]))
    ext_scr[pad:pad + tm] = u_ref[0]
    ext_scr[pad + tm:pad + tm + POOL_HALO] = jnp.where(i < last, un_ref[0], jnp.zeros_like(un_ref[0]))
    ext_scr[pad + tm + POOL_HALO:] = zeros
    for g, w in enumerate(POOL_WINDOWS):
        cols = slice(g * POOL_GROUP_DIM, (g + 1) * POOL_GROUP_DIM)
        uc = u_ref[0, :, cols].astype(F32)
        wsum = jnp.concatenate(
            [jnp.dot(band_ref[g], ext_scr[r0:r0 + POOL_SUB + 2 * pad, cols], preferred_element_type=F32)
             for r0 in range(0, tm, POOL_SUB)], axis=0)
        inv_mid = inv_mid_ref[:, cols]
        inv_head = jnp.where(i == 0, inv_first_ref[:, cols], inv_mid)
        inv_tail = jnp.where(i == last, inv_last_ref[:, cols], inv_mid)
        mean = jnp.concatenate([wsum[0:SUBLANES] * inv_head,
                                wsum[SUBLANES:tm - SUBLANES] * inv_mid,
                                wsum[tm - SUBLANES:] * inv_tail], axis=0)
        pooled = (mean - uc).astype(BF16)
        y_scr[:, cols] = jnp.dot(pooled, wg_ref[g], preferred_element_type=F32)

    yz = (y_scr[...] * ls_ref[...]).astype(BF16) * sz_ref[0]
    acc = jnp.dot(yz, wo_ref[...], preferred_element_type=F32)
    x2 = x_ref[0] + gate_ref[0] * acc
    ms = jnp.mean(x2 * x2, axis=-1, keepdims=True)
    o_ref[0] = x2 * lax.rsqrt(ms + RMS_EPS) * fn_ref[...]


def _pool_out(u, sz, x, mod, layer_scale, final_norm, w_grp, w_out, tm=512):
    b, s, d = x.shape
    per = tm // POOL_HALO
    nh = s // POOL_HALO
    inv_first, inv_last, inv_mid = _pool_edge_tables(s)
    tok = pl.BlockSpec((1, tm, d), lambda bi, i: (bi, i, 0))
    prev = pl.BlockSpec((1, POOL_HALO, d), lambda bi, i: (bi, jnp.maximum(i * per - 1, 0), 0))
    nxt = pl.BlockSpec((1, POOL_HALO, d), lambda bi, i: (bi, jnp.minimum((i + 1) * per, nh - 1), 0))
    row = lambda rows: _resident((rows, d), lambda bi, i: (0, 0))
    return pl.pallas_call(
        _pool_out_kernel,
        grid=(b, s // tm),
        in_specs=[tok, prev, nxt, tok, tok, _mod_spec(d, MOD_GATE),
                  row(1), row(1), row(SUBLANES), row(SUBLANES), row(1),
                  _resident((POOL_GROUPS, POOL_SUB, POOL_SUB + 2 * POOL_PAD), lambda bi, i: (0, 0, 0)),
                  _resident((POOL_GROUPS, POOL_GROUP_DIM, POOL_GROUP_DIM), lambda bi, i: (0, 0, 0)),
                  _resident((d, d), lambda bi, i: (0, 0))],
        out_specs=tok,
        out_shape=jax.ShapeDtypeStruct((b, s, d), F32),
        scratch_shapes=[pltpu.VMEM((tm + 2 * POOL_PAD, d), BF16), pltpu.VMEM((tm, d), F32)],
        compiler_params=_compiler_params(("parallel", "parallel")),
        name="pool_out",
    )(u, u, u, sz, x, mod, layer_scale.reshape(1, d), final_norm.reshape(1, d),
      jnp.asarray(inv_first), jnp.asarray(inv_last), jnp.asarray(inv_mid), jnp.asarray(_pool_band()).astype(BF16),
      w_grp, w_out)


def kernel(x, c, ctx, c_ctx, l0_norm, l0_w_mod, l0_b_mod, l0_w_in, l0_sink, l0_w_f, l0_w_out,
           l1_norm, l1_w_mod, l1_b_mod, l1_w_in, l1_w_grp, l1_scale, l1_w_out, final_norm):
    b, s, d = x.shape
    cond = jnp.concatenate([c, c_ctx[None, :], jnp.zeros((COND_ROWS - b - 1, d), F32)], axis=0)
    m0 = _modulation(cond, l0_w_mod, l0_b_mod)
    m1 = _modulation(cond, l1_w_mod, l1_b_mod)

    w_in0, kc, vc = _ctx_kv(ctx, l0_norm, m0, b, l0_w_in)
    cos, sin = _rope_tables(s)
    side = [l0_w_out, l1_w_in, l1_w_grp.reshape(POOL_GROUPS * POOL_GROUP_DIM, POOL_GROUP_DIM), l1_w_out]
    (q, k, v, f, sz0), (w_out0, w_in1, w_grp, w_out1) = _inproj0(
        x, l0_norm, m0, w_in0, jnp.asarray(cos), jnp.asarray(sin), side)
    attn = _attention(q, k, v, kc, vc, l0_sink)
    zf = _fourier_mix(f, l0_w_f)
    x1 = _outproj0(attn, zf, sz0, x, m0, w_out0)

    u, sz1 = _inproj1(x1, l1_norm, m1, w_in1)
    return _pool_out(u, sz1, x1, m1, l1_scale, final_norm, w_grp.reshape(l1_w_grp.shape), w_out1)
```
